```python
import math
import jax
import jax.numpy as jnp
from jax import lax
import numpy as np

D_MODEL = 1024
BATCH = 1
SEQ = 16384
DEPTH = 2

N_META = 16
GRID_W = 64
EPS = 1e-6
MLA_HEADS = D_MODEL // 128
MLA_NOPE = 64
MLA_ROPE = 32
MLA_V = 64
MLA_Q_RANK = D_MODEL // 4
MLA_KV_RANK = D_MODEL // 8
ROPE_THETA = 10000.0
Q_BLOCK = 128
NAT_HEADS = D_MODEL // 128
NAT_DIM = 64
WIN_ROWS = 8
WIN_COLS = 16
HY_WIDTH = D_MODEL // 2
HY_ORDER = 2
HY_SHORT = 3
HY_BANDS = 16
HY_EMB = 1 + 2 * HY_BANDS
HY_FFN = 64
HY_DIRS = 2
HY_TARGET = 1e-2
HY_MIN_RATE = math.log(1.0 / HY_TARGET) / 1.5
HY_MAX_RATE = math.log(1.0 / HY_TARGET) / 0.3
CF_WIDTH = D_MODEL // 2
CF_KERNEL = 31
MOE_GROUPS = 4
MOE_EXPERTS = 8
MOE_TOPK = 2
MOE_HIDDEN = D_MODEL // 4
A_IN_SIZES = (MLA_Q_RANK, MLA_KV_RANK, MLA_ROPE, NAT_HEADS * NAT_DIM, NAT_HEADS * NAT_DIM, NAT_HEADS * NAT_DIM)
A_OUT_WIDTH = MLA_HEADS * MLA_V + NAT_HEADS * NAT_DIM
C_IN_WIDTH = (HY_ORDER + 1) * HY_WIDTH + 2 * CF_WIDTH
C_OUT_WIDTH = HY_WIDTH + CF_WIDTH

kernel_name = 'hybrid_mla_natten_hyena_conformer_hmoe'


def split_cols(x, sizes):
    return jnp.split(x, [int(s) for s in np.cumsum(sizes)[:-1]], axis=-1)


def rmsnorm(x, g):
    x32 = x.astype(jnp.float32)
    y = x32 * lax.rsqrt(jnp.mean(x32 * x32, axis=-1, keepdims=True) + EPS)
    return (y * g.astype(jnp.float32)).astype(x.dtype)


def layernorm(x, g, b):
    x32 = x.astype(jnp.float32)
    mu = jnp.mean(x32, axis=-1, keepdims=True)
    xc = x32 - mu
    y = xc * lax.rsqrt(jnp.mean(xc * xc, axis=-1, keepdims=True) + EPS)
    return (y * g.astype(jnp.float32) + b.astype(jnp.float32)).astype(x.dtype)


def apply_rope(x, cos, sin):
    half = x.shape[-1] // 2
    x1, x2 = x[..., :half], x[..., half:]
    c = cos[None, :, None, :]
    s = sin[None, :, None, :]
    return jnp.concatenate([x1 * c - x2 * s, x1 * s + x2 * c], axis=-1).astype(x.dtype)


def depthwise_conv(x, w, b):
    k = w.shape[0]
    y = lax.conv_general_dilated(x, w[:, None, :].astype(x.dtype), window_strides=(1,), padding=[(k // 2, k // 2)],
                                 dimension_numbers=('NWC', 'WIO', 'NWC'), feature_group_count=x.shape[-1])
    return y + b.astype(x.dtype)


def dense_block_attention(q, k, v):
    B, T, H, dk = q.shape
    scale = dk ** -0.5

    def attend(qb):
        s = jnp.einsum('bqhd,bkhd->bhqk', qb, k, preferred_element_type=jnp.float32) * scale
        p = jax.nn.softmax(s, axis=-1).astype(v.dtype)
        return jnp.einsum('bhqk,bkhd->bqhd', p, v)

    out_meta = attend(q[:, :N_META])
    n_blk = (T - N_META) // Q_BLOCK
    qr = q[:, N_META:].reshape(B, n_blk, Q_BLOCK, H, dk).swapaxes(0, 1)
    out = lax.map(attend, qr)
    out = out.swapaxes(0, 1).reshape(B, n_blk * Q_BLOCK, H, v.shape[-1])
    return jnp.concatenate([out_meta, out], axis=1)


def neighbourhood_attention(q, k, v, rpb):
    B, T, H, d = q.shape
    scale = d ** -0.5
    n = T - N_META
    rows = n // GRID_W
    wr = min(WIN_ROWS, rows)
    qm, km, vm = q[:, :N_META], k[:, :N_META], v[:, :N_META]
    s_mm = jnp.einsum('bqhd,bkhd->bhqk', qm, km, preferred_element_type=jnp.float32) * scale
    out_meta = jnp.einsum('bhqk,bkhd->bqhd', jax.nn.softmax(s_mm, axis=-1).astype(v.dtype), vm)
    qg = q[:, N_META:].reshape(B, rows, GRID_W, H, d)
    kg = k[:, N_META:].reshape(B, rows, GRID_W, H, d)
    vg = v[:, N_META:].reshape(B, rows, GRID_W, H, d)
    col = np.arange(GRID_W)
    col_start = np.clip(col - WIN_COLS // 2, 0, GRID_W - WIN_COLS)
    col_idx = col_start[:, None] + np.arange(WIN_COLS)[None, :]
    col_bias_idx = (col_idx - col[:, None] + WIN_COLS - 1)[:, None, :]

    def row_block(r):
        r0 = jnp.clip(r - wr // 2, 0, rows - wr)
        kw = lax.dynamic_slice_in_dim(kg, r0, wr, axis=1)[:, :, col_idx]
        vw = lax.dynamic_slice_in_dim(vg, r0, wr, axis=1)[:, :, col_idx]
        qr = lax.dynamic_index_in_dim(qg, r, axis=1, keepdims=False)
        row_bias_idx = (r0 + jnp.arange(wr) - r + WIN_ROWS - 1)[None, :, None]
        bias = rpb[:, row_bias_idx, col_bias_idx]
        s_loc = jnp.einsum('bqhd,bwqchd->bhqwc', qr, kw, preferred_element_type=jnp.float32) * scale + bias[None]
        s_met = jnp.einsum('bqhd,bmhd->bhqm', qr, km, preferred_element_type=jnp.float32) * scale
        s = jnp.concatenate([s_met, s_loc.reshape(B, H, GRID_W, wr * WIN_COLS)], axis=-1)
        p = jax.nn.softmax(s, axis=-1).astype(v.dtype)
        p_loc = p[..., N_META:].reshape(B, H, GRID_W, wr, WIN_COLS)
        return (jnp.einsum('bhqm,bmhd->bqhd', p[..., :N_META], vm)
                + jnp.einsum('bhqwc,bwqchd->bqhd', p_loc, vw))

    out = lax.map(row_block, jnp.arange(rows))
    out = jnp.moveaxis(out, 0, 1).reshape(B, n, H, d)
    return jnp.concatenate([out_meta, out], axis=1)


def latent_neighbourhood_mixer(y, w_in, q_norm, w_uq, kv_norm, w_ukv, rpb, w_out, cos, sin):
    B, T, _ = y.shape
    c_q, c_kv, k_pe, nq, nk, nv = split_cols(y @ w_in, A_IN_SIZES)
    q = (rmsnorm(c_q, q_norm) @ w_uq).reshape(B, T, MLA_HEADS, MLA_NOPE + MLA_ROPE)
    kv = (rmsnorm(c_kv, kv_norm) @ w_ukv).reshape(B, T, MLA_HEADS, MLA_NOPE + MLA_V)
    q = jnp.concatenate([q[..., :MLA_NOPE], apply_rope(q[..., MLA_NOPE:], cos, sin)], axis=-1)
    k_pe = jnp.broadcast_to(apply_rope(k_pe[:, :, None, :], cos, sin), (B, T, MLA_HEADS, MLA_ROPE))
    k = jnp.concatenate([kv[..., :MLA_NOPE], k_pe], axis=-1)
    mla = dense_block_attention(q, k, kv[..., MLA_NOPE:])
    shp = (B, T, NAT_HEADS, NAT_DIM)
    nat = neighbourhood_attention(nq.reshape(shp), nk.reshape(shp), nv.reshape(shp), rpb)
    merged = jnp.concatenate([mla.reshape(B, T, -1), nat.reshape(B, T, -1)], axis=-1)
    return merged @ w_out


def hyena_filters(L, w1, b1, w2, b2, w3, sin_freq, log_decay):
    f32 = jnp.float32
    t = jnp.linspace(0.0, 1.0, L, dtype=f32)
    bands = jnp.linspace(1e-4, HY_BANDS - 1, HY_BANDS, dtype=f32)
    ang = 2.0 * math.pi * t[:, None] * bands[None, :]
    feats = jnp.concatenate([t[:, None], jnp.cos(ang), jnp.sin(ang)], axis=-1)
    hid = jnp.sin(sin_freq * (feats @ w1 + b1))
    hid = jnp.sin(sin_freq * (hid @ w2 + b2))
    filt = (hid @ w3).astype(f32).reshape(L, HY_DIRS, HY_ORDER, HY_WIDTH)
    filt = filt * jnp.exp(-t[:, None, None, None] * jnp.exp(log_decay.astype(f32))[None])
    fwd = filt[:, 0]
    bwd = filt[1:, 1][::-1]
    zero = jnp.zeros((1, HY_ORDER, HY_WIDTH), f32)
    return jnp.concatenate([fwd, zero, bwd], axis=0)


def fft_long_conv(z, k_freq, bias):
    L = z.shape[1]
    z32 = z.astype(jnp.float32)
    zf = jnp.fft.rfft(z32, n=2 * L, axis=1)
    y = jnp.fft.irfft(zf * k_freq[None], n=2 * L, axis=1)[:, :L]
    return (y + z32 * bias.astype(jnp.float32)).astype(z.dtype)


def hyena_conformer_mixer(y, w_in, short_w, short_b, f_w1, f_b1, f_w2, f_b2, f_w3, sin_freq, log_decay,
                          hy_bias, dw_w, dw_b, ln_g, ln_b, w_out):
    B, T, _ = y.shape
    p = y @ w_in
    hy_in, cf_in = p[..., :(HY_ORDER + 1) * HY_WIDTH], p[..., (HY_ORDER + 1) * HY_WIDTH:]
    v, *gates = jnp.split(depthwise_conv(hy_in, short_w, short_b), HY_ORDER + 1, axis=-1)
    k_freq = jnp.fft.rfft(hyena_filters(T, f_w1, f_b1, f_w2, f_b2, f_w3, sin_freq, log_decay), axis=0)
    z = v
    for o in range(HY_ORDER):
        z = gates[o] * fft_long_conv(z, k_freq[:, o], hy_bias[o])
    a, g = jnp.split(cf_in, 2, axis=-1)
    c = depthwise_conv(a * jax.nn.sigmoid(g), dw_w, dw_b)
    c = jax.nn.silu(layernorm(c, ln_g, ln_b))
    return jnp.concatenate([z, c], axis=-1) @ w_out


def hierarchical_moe(y, w_group, b_group, w_router, b_router, w1, w3, w2):
    B, T, D = y.shape
    f32 = jnp.float32
    xt = y.reshape(B * T, D)
    g_prob = jax.nn.softmax(jnp.einsum('nd,dg->ng', xt, w_group, preferred_element_type=f32) + b_group, axis=-1)
    p_group, g_idx = lax.top_k(g_prob, 1)
    g_onehot = jax.nn.one_hot(g_idx[:, 0], MOE_GROUPS, dtype=f32)
    e_logits_all = jnp.einsum('nd,gde->nge', xt, w_router, preferred_element_type=f32) + b_router
    e_logits = jnp.einsum('ng,nge->ne', g_onehot, e_logits_all)
    e_val, e_idx = lax.top_k(e_logits, MOE_TOPK)
    e_w = jax.nn.softmax(e_val, axis=-1) * p_group
    e_gate = jnp.einsum('nk,nke->ne', e_w, jax.nn.one_hot(e_idx, MOE_EXPERTS, dtype=f32))
    gate = (g_onehot[:, :, None] * e_gate[:, None, :]).astype(y.dtype)
    out = jnp.zeros((B * T, D), f32)
    for g in range(MOE_GROUPS):
        a = jnp.einsum('nd,edf->nef', xt, w1[g])
        b = jnp.einsum('nd,edf->nef', xt, w3[g])
        hid = jax.nn.silu(a) * b * gate[:, g, :, None]
        out = out + jnp.einsum('nef,efd->nd', hid, w2[g], preferred_element_type=f32)
    return out.astype(y.dtype).reshape(B, T, D)


def setup_inputs(seed: int = 0) -> dict:
    key = jax.random.key(seed)
    keys = iter(jax.random.split(key, 48))
    n_even = (DEPTH + 1) // 2
    n_odd = DEPTH // 2
    f32 = jnp.float32

    def normal(shape, scale):
        return jax.random.normal(next(keys), shape, f32) * scale

    def gain(shape):
        return 1.0 + normal(shape, 0.02)

    rates = jnp.log(jnp.linspace(HY_MIN_RATE, HY_MAX_RATE, HY_WIDTH, dtype=f32))
    hy_dec_shape = (n_odd, HY_DIRS, HY_ORDER, HY_WIDTH)
    return {
        'x': normal((BATCH, SEQ, D_MODEL), 1.0),
        'meta_tokens': normal((N_META, D_MODEL), 1.0),
        'norm_mix': gain((DEPTH, D_MODEL)),
        'norm_ffn': gain((DEPTH, D_MODEL)),
        'norm_final': gain((D_MODEL,)),
        'a_w_in': normal((n_even, D_MODEL, sum(A_IN_SIZES)), D_MODEL ** -0.5),
        'mla_q_norm': gain((n_even, MLA_Q_RANK)),
        'mla_w_uq': normal((n_even, MLA_Q_RANK, MLA_HEADS * (MLA_NOPE + MLA_ROPE)), MLA_Q_RANK ** -0.5),
        'mla_kv_norm': gain((n_even, MLA_KV_RANK)),
        'mla_w_ukv': normal((n_even, MLA_KV_RANK, MLA_HEADS * (MLA_NOPE + MLA_V)), MLA_KV_RANK ** -0.5),
        'nat_rpb': normal((n_even, NAT_HEADS, 2 * WIN_ROWS - 1, 2 * WIN_COLS - 1), 0.02),
        'a_w_out': normal((n_even, A_OUT_WIDTH, D_MODEL), A_OUT_WIDTH ** -0.5),
        'c_w_in': normal((n_odd, D_MODEL, C_IN_WIDTH), D_MODEL ** -0.5),
        'hy_short_w': normal((n_odd, HY_SHORT, (HY_ORDER + 1) * HY_WIDTH), HY_SHORT ** -0.5),
        'hy_short_b': normal((n_odd, (HY_ORDER + 1) * HY_WIDTH), 0.02),
        'hy_ffn_w1': normal((n_odd, HY_EMB, HY_FFN), HY_EMB ** -0.5),
        'hy_ffn_b1': normal((n_odd, HY_FFN), 0.02),
        'hy_ffn_w2': normal((n_odd, HY_FFN, HY_FFN), HY_FFN ** -0.5),
        'hy_ffn_b2': normal((n_odd, HY_FFN), 0.02),
        'hy_ffn_w3': normal((n_odd, HY_FFN, HY_DIRS * HY_ORDER * HY_WIDTH), 0.02 * HY_FFN ** -0.5),
        'hy_sin_freq': gain((n_odd, HY_FFN)),
        'hy_log_decay': jnp.broadcast_to(rates, hy_dec_shape) + normal(hy_dec_shape, 0.05),
        'hy_bias': normal((n_odd, HY_ORDER, HY_WIDTH), 0.5),
        'cf_dw_w': normal((n_odd, CF_KERNEL, CF_WIDTH), CF_KERNEL ** -0.5),
        'cf_dw_b': normal((n_odd, CF_WIDTH), 0.02),
        'cf_ln_g': gain((n_odd, CF_WIDTH)),
        'cf_ln_b': normal((n_odd, CF_WIDTH), 0.02),
        'c_w_out': normal((n_odd, C_OUT_WIDTH, D_MODEL), C_OUT_WIDTH ** -0.5),
        'moe_w_group': normal((DEPTH, D_MODEL, MOE_GROUPS), D_MODEL ** -0.5),
        'moe_b_group': normal((DEPTH, MOE_GROUPS), 0.01),
        'moe_w_router': normal((DEPTH, MOE_GROUPS, D_MODEL, MOE_EXPERTS), D_MODEL ** -0.5),
        'moe_b_router': normal((DEPTH, MOE_GROUPS, MOE_EXPERTS), 0.01),
        'moe_w1': normal((DEPTH, MOE_GROUPS, MOE_EXPERTS, D_MODEL, MOE_HIDDEN), D_MODEL ** -0.5),
        'moe_w3': normal((DEPTH, MOE_GROUPS, MOE_EXPERTS, D_MODEL, MOE_HIDDEN), D_MODEL ** -0.5),
        'moe_w2': normal((DEPTH, MOE_GROUPS, MOE_EXPERTS, MOE_HIDDEN, D_MODEL), MOE_HIDDEN ** -0.5),
    }


def reference(x, meta_tokens, norm_mix, norm_ffn, norm_final, a_w_in, mla_q_norm, mla_w_uq, mla_kv_norm,
              mla_w_ukv, nat_rpb, a_w_out, c_w_in, hy_short_w, hy_short_b, hy_ffn_w1, hy_ffn_b1, hy_ffn_w2,
              hy_ffn_b2, hy_ffn_w3, hy_sin_freq, hy_log_decay, hy_bias, cf_dw_w, cf_dw_b, cf_ln_g, cf_ln_b,
              c_w_out, moe_w_group, moe_b_group, moe_w_router, moe_b_router, moe_w1, moe_w3, moe_w2):
    B = x.shape[0]
    meta = jnp.broadcast_to(meta_tokens[None].astype(x.dtype), (B, N_META, D_MODEL))
    h = jnp.concatenate([meta, x], axis=1)
    T = h.shape[1]
    pos = jnp.arange(T, dtype=jnp.float32)
    inv_freq = ROPE_THETA ** (-jnp.arange(0, MLA_ROPE, 2, dtype=jnp.float32) / MLA_ROPE)
    ang = pos[:, None] * inv_freq[None, :]
    cos, sin = jnp.cos(ang), jnp.sin(ang)
    for layer in range(DEPTH):
        i = layer // 2
        y = rmsnorm(h, norm_mix[layer])
        if layer % 2 == 0:
            y = latent_neighbourhood_mixer(y, a_w_in[i], mla_q_norm[i], mla_w_uq[i], mla_kv_norm[i],
                                           mla_w_ukv[i], nat_rpb[i], a_w_out[i], cos, sin)
        else:
            y = hyena_conformer_mixer(y, c_w_in[i], hy_short_w[i], hy_short_b[i], hy_ffn_w1[i], hy_ffn_b1[i],
                                      hy_ffn_w2[i], hy_ffn_b2[i], hy_ffn_w3[i], hy_sin_freq[i],
                                      hy_log_decay[i], hy_bias[i], cf_dw_w[i], cf_dw_b[i], cf_ln_g[i],
                                      cf_ln_b[i], c_w_out[i])
        h = h + y
        h = h + hierarchical_moe(rmsnorm(h, norm_ffn[layer]), moe_w_group[layer], moe_b_group[layer],
                                 moe_w_router[layer], moe_b_router[layer], moe_w1[layer], moe_w3[layer],
                                 moe_w2[layer])
    h = rmsnorm(h, norm_final)
    return h[:, N_META:]
```

```python
import functools
import math

import numpy as np
import jax
import jax.numpy as jnp
from jax import lax
from jax.experimental import pallas as pl
from jax.experimental.pallas import tpu as pltpu

f32 = jnp.float32
bf16 = jnp.bfloat16

D_MODEL = 1024
SEQ = 16384
N_META = 16
T = SEQ + N_META
TM = 512
TP = 33 * TM
NB = TP // TM
EPS = 1e-6
HEADS = 8
MLA_DK = 96
GRID_W = 64
ROWS = SEQ // GRID_W
WIN_ROWS = 8
WIN_COLS = 16
NEG = -1e30

HY_W = 512
CF_K = 31
N2 = 128
N1 = 286
NFFT = N1 * N2
K1 = N1 // 2 + 1
N1P = 144
THY = N1P * N2

VMEM_LIMIT = 56 * 1024 * 1024


def _cp(sem, vmem=VMEM_LIMIT):
    return pltpu.CompilerParams(dimension_semantics=sem, vmem_limit_bytes=vmem)


def _rms(x, g):
    return x * lax.rsqrt(jnp.mean(x * x, axis=-1, keepdims=True) + EPS) * g


def _dot(a, b):
    return jnp.dot(a, b, preferred_element_type=f32)


def _dot_hi(a, b):
    return jnp.dot(a, b, preferred_element_type=f32, precision=lax.Precision.HIGHEST)


def _k_in0(h_ref, g_ref, win_ref, qn_ref, kvn_ref, wq_ref, wqr_ref, wk_ref, e_ref, er_ref, wv_ref, cos_ref, sin_ref,
           q_out, k_out, v_out, nq_out, nk_out, nv_out):
    hn = _rms(h_ref[...], g_ref[...]).astype(bf16)
    p = _dot(hn, win_ref[...])
    nq_out[...] = p[:, 512:1024].astype(bf16)
    nk_out[...] = p[:, 1024:1536].astype(bf16)
    nv_out[...] = p[:, 1536:2048].astype(bf16)
    cqn = _rms(p[:, 0:256], qn_ref[...]).astype(bf16)
    ckvn = _rms(p[:, 256:384], kvn_ref[...]).astype(bf16)
    kpe = p[:, 384:512].astype(bf16)
    qa = _dot(cqn, wq_ref[...])
    qr = _dot(cqn, wqr_ref[...])
    ka = _dot(ckvn, wk_ref[...]) + _dot(kpe, e_ref[...])
    kr = _dot(kpe, er_ref[...])
    v_out[...] = _dot(ckvn, wv_ref[...]).astype(bf16)
    cos = cos_ref[...]
    sin = sin_ref[...]
    for hd in range(HEADS):
        sl = slice(hd * 128, (hd + 1) * 128)
        q_out[:, sl] = (qa[:, sl] * cos + qr[:, sl] * sin).astype(bf16)
        k_out[:, sl] = (ka[:, sl] * cos + kr[:, sl] * sin).astype(bf16)


def _in0(h, g, w):
    full = lambda shape: pl.BlockSpec(shape, lambda i: (0,) * len(shape))
    row = lambda n: pl.BlockSpec((TM, n), lambda i: (i, 0))
    outs = [jax.ShapeDtypeStruct((TP, n), bf16) for n in (1024, 1024, 512, 512, 512, 512)]
    return pl.pallas_call(
        _k_in0,
        grid=(NB,),
        in_specs=[row(1024), full((1, 1024)), full((1024, 2048)), full((1, 256)), full((1, 128)),
                  full((256, 1024)), full((256, 1024)), full((128, 1024)), full((128, 1024)), full((128, 1024)),
                  full((128, 512)), row(128), row(128)],
        out_specs=[row(1024), row(1024), row(512), row(512), row(512), row(512)],
        out_shape=outs,
        compiler_params=_cp(("parallel",)),
        name="in0",
    )(h, g, w["win"], w["qn"], w["kvn"], w["wq"], w["wqr"], w["wk"], w["e"], w["er"], w["wv"], w["cos"], w["sin"])


def _k_mla(q_ref, k_ref, v_ref, o_ref):
    q = q_ref[0]

    def step(j, carry, masked):
        m, l, acc = carry
        off = pl.multiple_of(j * TM, TM)
        k = k_ref[0, pl.ds(off, TM), :]
        v = v_ref[0, pl.ds(off, TM), :]
        s = lax.dot_general(q, k, (((1,), (1,)), ((), ())), preferred_element_type=f32)
        if masked:
            col = lax.broadcasted_iota(jnp.int32, (1, TM), 1)
            s = jnp.where(col < T - (NB - 1) * TM, s, NEG)
        m_new = jnp.maximum(m, s.max(axis=1, keepdims=True))
        alpha = jnp.exp(m - m_new)
        p = jnp.exp(s - m_new)
        l = alpha * l + p.sum(axis=1, keepdims=True)
        acc = alpha * acc + _dot(p.astype(bf16), v)
        return m_new, l, acc

    init = (jnp.full((TM, 1), NEG, f32), jnp.zeros((TM, 1), f32), jnp.zeros((TM, 64), f32))
    carry = lax.fori_loop(0, NB - 1, functools.partial(step, masked=False), init)
    _, l, acc = step(NB - 1, carry, True)
    o_ref[0] = (acc / l).astype(bf16)


def _mla(q, k, v):
    return pl.pallas_call(
        _k_mla,
        grid=(HEADS, NB),
        in_specs=[pl.BlockSpec((1, TM, 128), lambda h, i: (h, i, 0)),
                  pl.BlockSpec((1, TP, 128), lambda h, i: (h, 0, 0)),
                  pl.BlockSpec((1, TP, 64), lambda h, i: (h, 0, 0))],
        out_specs=pl.BlockSpec((1, TM, 64), lambda h, i: (h, i, 0)),
        out_shape=jax.ShapeDtypeStruct((HEADS, TP, 64), bf16),
        compiler_params=_cp(("parallel", "parallel")),
        name="mla",
    )(q, k, v)


def _k_nat(q_ref, k_ref, v_ref, c_ref, o_ref):
    km = k_ref[0, 0:N_META, :]
    vm = v_ref[0, 0:N_META, :]
    qm = q_ref[0, 0:N_META, :]
    s = lax.dot_general(qm, km, (((1,), (1,)), ((), ())), preferred_element_type=f32)
    p = jnp.exp(s - s.max(axis=1, keepdims=True))
    o_ref[0, 0:N_META, :] = (_dot(p.astype(bf16), vm) / p.sum(axis=1, keepdims=True)).astype(bf16)
    o_ref[0, T:TP, :] = jnp.zeros((TP - T, 64), bf16)

    def row(i, _):
        r0 = jnp.clip(i - WIN_ROWS // 2, 0, ROWS - WIN_ROWS)
        a0 = r0 - i + WIN_ROWS - 1
        qoff = pl.multiple_of(N_META + GRID_W * i, 16)
        koff = pl.multiple_of(N_META + GRID_W * r0, 16)
        qi = q_ref[0, pl.ds(qoff, GRID_W), :]
        kw = k_ref[0, pl.ds(koff, WIN_ROWS * GRID_W), :]
        vw = v_ref[0, pl.ds(koff, WIN_ROWS * GRID_W), :]
        s = lax.dot_general(qi, kw, (((1,), (1,)), ((), ())), preferred_element_type=f32)
        s = s + jnp.concatenate([c_ref[0, a0 + 2 * j] for j in range(WIN_ROWS // 2)], axis=1)
        sm = lax.dot_general(qi, km, (((1,), (1,)), ((), ())), preferred_element_type=f32)
        m = jnp.maximum(s.max(axis=1, keepdims=True), sm.max(axis=1, keepdims=True))
        p = jnp.exp(s - m)
        pm = jnp.exp(sm - m)
        l = p.sum(axis=1, keepdims=True) + pm.sum(axis=1, keepdims=True)
        o = _dot(p.astype(bf16), vw) + _dot(pm.astype(bf16), vm)
        o_ref[0, pl.ds(qoff, GRID_W), :] = (o / l).astype(bf16)
        return 0

    lax.fori_loop(0, ROWS, row, 0)


def _nat(q, k, v, c2):
    blk = pl.BlockSpec((1, TP, 64), lambda h: (h, 0, 0))
    return pl.pallas_call(
        _k_nat,
        grid=(HEADS,),
        in_specs=[blk, blk, blk, pl.BlockSpec((1, 2 * WIN_ROWS - 2, GRID_W, 2 * GRID_W), lambda h: (h, 0, 0, 0))],
        out_specs=blk,
        out_shape=jax.ShapeDtypeStruct((HEADS, TP, 64), bf16),
        compiler_params=_cp(("parallel",)),
        name="nat",
    )(q, k, v, c2)


def _k_out_router(h_ref, a_ref, b_ref, wa_ref, wb_ref, g_ref, wr_ref, br_ref, h_out, xn_out, gate_out):
    h1 = h_ref[...] + _dot(a_ref[...].astype(bf16), wa_ref[...]) + _dot(b_ref[...].astype(bf16), wb_ref[...])
    h_out[...] = h1
    xn = _rms(h1, g_ref[...])
    xn_out[...] = xn.astype(bf16)
    logits = _dot_hi(xn, wr_ref[...]) + br_ref[...]
    lane = lax.broadcasted_iota(jnp.int32, logits.shape, 1)
    big = jnp.int32(1 << 20)
    gl = jnp.where((lane >= 32) & (lane < 36), logits, NEG)
    gmax = gl.max(axis=1, keepdims=True)
    gsum = jnp.exp(gl - gmax).sum(axis=1, keepdims=True)
    p_group = 1.0 / gsum
    gidx = jnp.where(gl == gmax, lane, big).min(axis=1, keepdims=True) - 32
    el = jnp.where((lane >> 3) == gidx, logits, NEG)
    v1 = el.max(axis=1, keepdims=True)
    i1 = jnp.where(el == v1, lane, big).min(axis=1, keepdims=True)
    el2 = jnp.where(lane == i1, NEG, el)
    v2 = el2.max(axis=1, keepdims=True)
    i2 = jnp.where(el2 == v2, lane, big).min(axis=1, keepdims=True)
    e2 = jnp.exp(v2 - v1)
    w1 = p_group / (1.0 + e2)
    w2 = w1 * e2
    gate_out[...] = jnp.where(lane == i1, w1, jnp.where(lane == i2, w2, 0.0))


def _out_router(h, a, b, wa, wb, g, wr, br):
    full = lambda shape: pl.BlockSpec(shape, lambda i: (0,) * len(shape))
    row = lambda n: pl.BlockSpec((TM, n), lambda i: (i, 0))
    return pl.pallas_call(
        _k_out_router,
        grid=(NB,),
        in_specs=[row(1024), row(512), row(512), full((512, 1024)), full((512, 1024)), full((1, 1024)),
                  full((1024, 128)), full((1, 128))],
        out_specs=[row(1024), row(1024), row(128)],
        out_shape=[jax.ShapeDtypeStruct((TP, 1024), f32), jax.ShapeDtypeStruct((TP, 1024), bf16),
                   jax.ShapeDtypeStruct((TP, 128), f32)],
        compiler_params=_cp(("parallel",)),
        name="out_router",
    )(h, a, b, wa, wb, g, wr, br)


def _k_moe(xn_ref, gate_ref, h_ref, w13_ref, w2_ref, ex_ref, o_ref):
    first = (pl.program_id(1) == 0) & (pl.program_id(2) == 0)

    @pl.when(first)
    def _():
        o_ref[...] = h_ref[...]

    a = _dot(xn_ref[...], w13_ref[0, 0])
    gate = gate_ref[...]
    ghi = gate.astype(bf16)
    glo = (gate - ghi.astype(f32)).astype(bf16)
    gx = _dot(ghi, ex_ref[0, 0]) + _dot(glo, ex_ref[0, 0])
    a1 = a[:, :1024]
    hid = a1 * jax.nn.sigmoid(a1) * a[:, 1024:] * gx
    o_ref[...] += _dot(hid.astype(bf16), w2_ref[0, 0])


def _moe(xn, gate, h, w13, w2, ex):
    row = lambda n: pl.BlockSpec((TM, n), lambda i, g, s: (i, 0))
    return pl.pallas_call(
        _k_moe,
        grid=(NB, 4, 2),
        in_specs=[row(1024), row(128), row(1024),
                  pl.BlockSpec((1, 1, 1024, 2048), lambda i, g, s: (g, s, 0, 0)),
                  pl.BlockSpec((1, 1, 1024, 1024), lambda i, g, s: (g, s, 0, 0)),
                  pl.BlockSpec((1, 1, 128, 1024), lambda i, g, s: (g, s, 0, 0))],
        out_specs=row(1024),
        out_shape=jax.ShapeDtypeStruct((TP, 1024), f32),
        compiler_params=_cp(("parallel", "arbitrary", "arbitrary")),
        name="moe",
    )(xn, gate, h, w13, w2, ex)


def _k_in1(h_ref, g_ref, w_ref, hy_out, cf_out):
    hn = _rms(h_ref[...], g_ref[...]).astype(bf16)
    p = _dot(hn, w_ref[...])
    hy_out[...] = p[:, :1536]
    cf_out[...] = p[:, 1536:]


def _in1(h, g, w):
    row = lambda n: pl.BlockSpec((TM, n), lambda i: (i, 0))
    return pl.pallas_call(
        _k_in1,
        grid=(NB,),
        in_specs=[row(1024), pl.BlockSpec((1, 1024), lambda i: (0, 0)), pl.BlockSpec((1024, 2560), lambda i: (0, 0))],
        out_specs=[row(1536), row(1024)],
        out_shape=[jax.ShapeDtypeStruct((TP, 1536), f32), jax.ShapeDtypeStruct((TP, 1024), f32)],
        compiler_params=_cp(("parallel",)),
        name="in1",
    )(h, g, w)


def _k_short(prev_ref, cur_ref, next_ref, w_ref, b_ref, v_out, x1_out, x2_out):
    i = pl.program_id(0)
    ext = jnp.concatenate([prev_ref[...], cur_ref[...], next_ref[...]], axis=0)
    t = i * TM - 8 + lax.broadcasted_iota(jnp.int32, (TM + 16, 1), 0)
    ext = jnp.where((t >= 0) & (t < T), ext, 0.0)
    w = w_ref[...]
    s = ext[7:7 + TM] * w[0:1] + ext[8:8 + TM] * w[1:2] + ext[9:9 + TM] * w[2:3] + b_ref[...]
    s = jnp.where(t[8:8 + TM] < T, s, 0.0)
    v_out[...] = s[:, :512]
    x1_out[...] = s[:, 512:1024]
    x2_out[...] = s[:, 1024:]


def _short(hy, w, b):
    last = NB - 1
    nblk = THY // TM
    return pl.pallas_call(
        _k_short,
        grid=(nblk,),
        in_specs=[pl.BlockSpec((8, 1536), lambda i: (jnp.maximum(jnp.minimum(i, last) * (TM // 8) - 1, 0), 0)),
                  pl.BlockSpec((TM, 1536), lambda i: (jnp.minimum(i, last), 0)),
                  pl.BlockSpec((8, 1536), lambda i: (jnp.minimum((jnp.minimum(i, last) + 1) * (TM // 8), TP // 8 - 1), 0)),
                  pl.BlockSpec((3, 1536), lambda i: (0, 0)), pl.BlockSpec((1, 1536), lambda i: (0, 0))],
        out_specs=[pl.BlockSpec((TM, 512), lambda i: (i, 0))] * 3,
        out_shape=[jax.ShapeDtypeStruct((THY, 512), f32)] * 3,
        compiler_params=_cp(("parallel",)),
        name="hy_short",
    )(hy, hy, hy, w, b)


def _k_filt(feat_ref, w1_ref, b1_ref, w2_ref, b2_ref, w3_ref, fr_ref, rate_ref, o_ref):
    i = pl.program_id(0)
    feats = feat_ref[...]
    fr = fr_ref[...]
    hid = jnp.sin(fr * (_dot_hi(feats, w1_ref[...]) + b1_ref[...]))
    hid = jnp.sin(fr * (_dot_hi(hid, w2_ref[...]) + b2_ref[...]))
    filt = _dot_hi(hid, w3_ref[...])
    filt = filt * jnp.exp(-feats[:, 0:1] * rate_ref[...])
    lag = i * TM + lax.broadcasted_iota(jnp.int32, (TM, 1), 0)
    col = lax.broadcasted_iota(jnp.int32, (1, 2048), 1)
    keep = (lag < T) & ((lag > 0) | (col < 1024))
    o_ref[...] = jnp.where(keep, filt, 0.0)


def _filt(feats, w1, b1, w2, b2, w3, fr, rate):
    full = lambda shape: pl.BlockSpec(shape, lambda i: (0,) * len(shape))
    return pl.pallas_call(
        _k_filt,
        grid=(THY // TM,),
        in_specs=[pl.BlockSpec((TM, 128), lambda i: (i, 0)), full((128, 64)), full((1, 64)), full((64, 64)),
                  full((1, 64)), full((64, 2048)), full((1, 64)), full((1, 2048))],
        out_specs=pl.BlockSpec((TM, 2048), lambda i: (i, 0)),
        out_shape=jax.ShapeDtypeStruct((THY, 2048), f32),
        compiler_params=_cp(("parallel",)),
        name="hy_filt",
    )(feats, w1, b1, w2, b2, w3, fr, rate)


def _k_dft1(f_ref, x_ref, y_ref):
    f = f_ref[...]
    for j in range(8):
        y = _dot(f, x_ref[:, j, :].astype(bf16))
        y_ref[:, 0, j, :] = y[:K1]
        y_ref[:, 1, j, :] = y[K1:]


def _dft1(f1c, x):
    c = x.shape[1]
    x3 = x.reshape(N1P, N2, c)
    return pl.pallas_call(
        _k_dft1,
        grid=(N2 // 8, c // 512),
        in_specs=[pl.BlockSpec((2 * K1, N1P), lambda i, cb: (0, 0)),
                  pl.BlockSpec((N1P, 8, 512), lambda i, cb: (0, i, cb))],
        out_specs=pl.BlockSpec((K1, 2, 8, 512), lambda i, cb: (0, 0, i, cb)),
        out_shape=jax.ShapeDtypeStruct((K1, 2, N2, c), f32),
        compiler_params=_cp(("parallel", "parallel")),
        name="dft1",
    )(f1c, x3)


def _k_spec_filt(y_ref, g_ref, o_ref):
    y = y_ref[0].reshape(2 * N2, 2048).astype(bf16)
    x = _dot(g_ref[0], y)
    o_ref[0, 0] = x[:N2, :1024] + x[:N2, 1024:]
    o_ref[0, 1] = x[N2:, :1024] - x[N2:, 1024:]


def _spec_filt(yf, gblk):
    return pl.pallas_call(
        _k_spec_filt,
        grid=(K1,),
        in_specs=[pl.BlockSpec((1, 2, N2, 2048), lambda k: (k, 0, 0, 0)),
                  pl.BlockSpec((1, 2 * N2, 2 * N2), lambda k: (k, 0, 0))],
        out_specs=pl.BlockSpec((1, 2, N2, 1024), lambda k: (k, 0, 0, 0)),
        out_shape=jax.ShapeDtypeStruct((K1, 2, N2, 1024), f32),
        compiler_params=_cp(("parallel",)),
        name="spec_filt",
    )(yf, gblk)


def _k_spec_conv(y_ref, kf_ref, g_ref, gi_ref, z_ref):
    y = y_ref[0].reshape(2 * N2, 512).astype(bf16)
    x = _dot(g_ref[0], y)
    xr, xi = x[:N2], x[N2:]
    kr, ki = kf_ref[0, 0], kf_ref[0, 1]
    p = jnp.concatenate([xr * kr - xi * ki, xr * ki + xi * kr], axis=0).astype(bf16)
    z_ref[0] = _dot(gi_ref[0], p).reshape(2, N2, 512)


def _spec_conv(y, kf, order, gblk, ginv):
    return pl.pallas_call(
        _k_spec_conv,
        grid=(K1,),
        in_specs=[pl.BlockSpec((1, 2, N2, 512), lambda k: (k, 0, 0, 0)),
                  pl.BlockSpec((1, 2, N2, 512), lambda k: (k, 0, 0, order)),
                  pl.BlockSpec((1, 2 * N2, 2 * N2), lambda k: (k, 0, 0)),
                  pl.BlockSpec((1, 2 * N2, 2 * N2), lambda k: (k, 0, 0))],
        out_specs=pl.BlockSpec((1, 2, N2, 512), lambda k: (k, 0, 0, 0)),
        out_shape=jax.ShapeDtypeStruct((K1, 2, N2, 512), f32),
        compiler_params=_cp(("parallel",)),
        name="spec_conv",
    )(y, kf, gblk, ginv)


def _k_idft(f_ref, z_ref, zin_ref, gate_ref, bias_ref, o_ref):
    f = f_ref[...]
    bias = bias_ref[...]
    for j in range(8):
        zj = jnp.concatenate([z_ref[:, 0, j, :], z_ref[:, 1, j, :]], axis=0).astype(bf16)
        y = _dot(f, zj)
        o_ref[:, j, :] = gate_ref[:, j, :] * (y + zin_ref[:, j, :] * bias)


def _idft(f1inv, z, zin, gate, bias):
    blk = pl.BlockSpec((N1P, 8, 512), lambda i: (0, i, 0))
    out = pl.pallas_call(
        _k_idft,
        grid=(N2 // 8,),
        in_specs=[pl.BlockSpec((N1P, 2 * K1), lambda i: (0, 0)),
                  pl.BlockSpec((K1, 2, 8, 512), lambda i: (0, 0, i, 0)), blk, blk,
                  pl.BlockSpec((1, 512), lambda i: (0, 0))],
        out_specs=blk,
        out_shape=jax.ShapeDtypeStruct((N1P, N2, 512), f32),
        compiler_params=_cp(("parallel",)),
        name="idft",
    )(f1inv, z, zin.reshape(N1P, N2, 512), gate.reshape(N1P, N2, 512), bias)
    return out.reshape(THY, 512)


def _k_conf(prev_ref, cur_ref, next_ref, w_ref, b_ref, lg_ref, lb_ref, o_ref):
    i = pl.program_id(0)
    ext = jnp.concatenate([prev_ref[...], cur_ref[...], next_ref[...]], axis=0)
    t = i * TM - 16 + lax.broadcasted_iota(jnp.int32, (TM + 32, 1), 0)
    u = ext[:, :512] * jax.nn.sigmoid(ext[:, 512:])
    u = jnp.where((t >= 0) & (t < T), u, 0.0)
    w = w_ref[...]
    acc = jnp.zeros((TM, 512), f32) + b_ref[...]
    for j in range(CF_K):
        acc = acc + u[j + 1:j + 1 + TM] * w[j:j + 1]
    mu = jnp.mean(acc, axis=-1, keepdims=True)
    xc = acc - mu
    y = xc * lax.rsqrt(jnp.mean(xc * xc, axis=-1, keepdims=True) + EPS) * lg_ref[...] + lb_ref[...]
    o_ref[...] = (y * jax.nn.sigmoid(y)).astype(bf16)


def _conf(cf, w, b, lg, lb):
    full = lambda shape: pl.BlockSpec(shape, lambda i: (0,) * len(shape))
    return pl.pallas_call(
        _k_conf,
        grid=(NB,),
        in_specs=[pl.BlockSpec((16, 1024), lambda i: (jnp.maximum(i * (TM // 16) - 1, 0), 0)),
                  pl.BlockSpec((TM, 1024), lambda i: (i, 0)),
                  pl.BlockSpec((16, 1024), lambda i: (jnp.minimum((i + 1) * (TM // 16), TP // 16 - 1), 0)),
                  full((CF_K, 512)), full((1, 512)), full((1, 512)), full((1, 512))],
        out_specs=pl.BlockSpec((TM, 512), lambda i: (i, 0)),
        out_shape=jax.ShapeDtypeStruct((TP, 512), bf16),
        compiler_params=_cp(("parallel",)),
        name="conformer",
    )(cf, cf, cf, w, b, lg, lb)


def _k_final(h_ref, g_ref, o_ref):
    o_ref[...] = _rms(h_ref[...], g_ref[...])


def _final(h, g):
    return pl.pallas_call(
        _k_final,
        grid=(NB,),
        in_specs=[pl.BlockSpec((TM, 1024), lambda i: (i, 0)), pl.BlockSpec((1, 1024), lambda i: (0, 0))],
        out_specs=pl.BlockSpec((TM, 1024), lambda i: (i, 0)),
        out_shape=jax.ShapeDtypeStruct((TP, 1024), f32),
        compiler_params=_cp(("parallel",)),
        name="final_norm",
    )(h, g)


def _prep_in0(a_w_in, q_norm, w_uq, kv_norm, w_ukv):
    win = jnp.concatenate([a_w_in[:, :416], jnp.zeros((D_MODEL, 96), f32), a_w_in[:, 416:928] * (64 ** -0.5),
                           a_w_in[:, 928:]], axis=1).astype(bf16)
    wq3 = (w_uq * (MLA_DK ** -0.5)).reshape(256, HEADS, MLA_DK)
    rope = wq3[..., 64:]
    rot = jnp.concatenate([-rope[..., 16:], rope[..., :16]], axis=-1)
    z64 = jnp.zeros((256, HEADS, 64), f32)
    z32 = jnp.zeros((256, HEADS, 32), f32)
    wq = jnp.concatenate([wq3, z32], axis=-1).reshape(256, 1024).astype(bf16)
    wqr = jnp.concatenate([z64, rot, z32], axis=-1).reshape(256, 1024).astype(bf16)
    kv3 = w_ukv.reshape(128, HEADS, 128)
    wk = jnp.concatenate([kv3[..., :64], jnp.zeros((128, HEADS, 64), f32)], axis=-1).reshape(128, 1024).astype(bf16)
    wv = kv3[..., 64:].reshape(128, 512).astype(bf16)
    e = np.zeros((128, HEADS, 128), np.float32)
    er = np.zeros((128, HEADS, 128), np.float32)
    for r in range(32):
        e[r, :, 64 + r] = 1.0
        if r < 16:
            er[r + 16, :, 64 + r] = -1.0
        else:
            er[r - 16, :, 64 + r] = 1.0
    pos = jnp.arange(TP, dtype=f32)
    inv_freq = 10000.0 ** (-jnp.arange(0, 32, 2, dtype=f32) / 32)
    ang = pos[:, None] * inv_freq[None, :]
    c, s = jnp.cos(ang), jnp.sin(ang)
    cos = jnp.concatenate([jnp.ones((TP, 64), f32), c, c, jnp.zeros((TP, 32), f32)], axis=1)
    sin = jnp.concatenate([jnp.zeros((TP, 64), f32), s, s, jnp.zeros((TP, 32), f32)], axis=1)
    return dict(win=win, qn=q_norm[None], kvn=kv_norm[None], wq=wq, wqr=wqr, wk=wk,
                e=jnp.asarray(e.reshape(128, 1024)).astype(bf16), er=jnp.asarray(er.reshape(128, 1024)).astype(bf16),
                wv=wv, cos=cos, sin=sin)


def _nat_bias(rpb):
    qc = np.arange(GRID_W)[:, None]
    kc = np.arange(GRID_W)[None, :]
    c0 = np.clip(qc - WIN_COLS // 2, 0, GRID_W - WIN_COLS)
    inside = (kc >= c0) & (kc < c0 + WIN_COLS)
    idx = np.clip(kc - qc + WIN_COLS - 1, 0, 2 * WIN_COLS - 2)
    c = jnp.where(jnp.asarray(inside)[None, None], rpb[:, :, idx], NEG)
    return jnp.concatenate([c[:, :-1], c[:, 1:]], axis=-1)


def _prep_moe(w_group, b_group, w_router, b_router, w1, w3, w2):
    wr = jnp.concatenate([w_router.transpose(1, 0, 2).reshape(D_MODEL, 32), w_group,
                          jnp.zeros((D_MODEL, 92), f32)], axis=1)
    br = jnp.concatenate([b_router.reshape(32), b_group, jnp.zeros((92,), f32)])[None]

    def half(w):
        return w.reshape(4, 2, 4, D_MODEL, 256).transpose(0, 1, 3, 2, 4).reshape(4, 2, D_MODEL, 1024)

    w13 = jnp.concatenate([half(w1), half(w3)], axis=-1).astype(bf16)
    w2h = w2.reshape(4, 2, 1024, D_MODEL).astype(bf16)
    ex = np.zeros((4, 2, 128, 4, 256), np.float32)
    for g in range(4):
        for s in range(2):
            for e in range(4):
                ex[g, s, g * 8 + s * 4 + e, e, :] = 1.0
    return wr, br, w13, w2h, jnp.asarray(ex.reshape(4, 2, 128, 1024)).astype(bf16)


def _dft_tables():
    two_pi = 2.0 * math.pi
    k1 = jnp.arange(K1, dtype=jnp.int32)
    n1 = jnp.arange(N1P, dtype=jnp.int32)
    th = two_pi * ((k1[:, None] * n1[None, :]) % N1).astype(f32) / N1
    f1c = jnp.concatenate([jnp.cos(th), -jnp.sin(th)], axis=0).astype(bf16)
    wgt = jnp.where((k1 == 0) | (k1 == N1 // 2), 1.0, 2.0).astype(f32) / NFFT
    f1inv = jnp.concatenate([jnp.cos(th).T * wgt[None, :], -jnp.sin(th).T * wgt[None, :]], axis=1).astype(bf16)
    k2 = jnp.arange(N2, dtype=jnp.int32)
    n2 = jnp.arange(N2, dtype=jnp.int32)
    k = k1[:, None, None] + N1 * k2[None, :, None]
    ang = -two_pi * ((k * n2[None, None, :]) % NFFT).astype(f32) / NFFT
    gre, gim = jnp.cos(ang), jnp.sin(ang)
    gblk = jnp.concatenate([jnp.concatenate([gre, -gim], axis=2), jnp.concatenate([gim, gre], axis=2)], axis=1)
    gre_t, gim_t = gre.transpose(0, 2, 1), gim.transpose(0, 2, 1)
    ginv = jnp.concatenate([jnp.concatenate([gre_t, gim_t], axis=2), jnp.concatenate([-gim_t, gre_t], axis=2)], axis=1)
    return f1c, f1inv, gblk.astype(bf16), ginv.astype(bf16)


def _hyena(hy_in, short_w, short_b, f_w1, f_b1, f_w2, f_b2, f_w3, sin_freq, log_decay, hy_bias):
    v, x1, x2 = _short(hy_in, short_w, short_b[None])
    f1c, f1inv, gblk, ginv = _dft_tables()
    tt = jnp.linspace(0.0, 1.0, T, dtype=f32)
    bands = jnp.linspace(1e-4, 15.0, 16, dtype=f32)
    ang = 2.0 * math.pi * tt[:, None] * bands[None, :]
    feats = jnp.concatenate([tt[:, None], jnp.cos(ang), jnp.sin(ang), jnp.zeros((T, 95), f32)], axis=-1)
    feats = jnp.concatenate([feats, jnp.zeros((THY - T, 128), f32)], axis=0)
    w1p = jnp.concatenate([f_w1, jnp.zeros((95, 64), f32)], axis=0)
    rate = jnp.exp(log_decay.astype(f32)).reshape(1, 2048)
    filt = _filt(feats, w1p, f_b1[None], f_w2, f_b2[None], f_w3, sin_freq[None], rate)
    kf = _spec_filt(_dft1(f1c, filt), gblk)
    z = v
    for o, gate in enumerate((x1, x2)):
        zf = _spec_conv(_dft1(f1c, z), kf, o, gblk, ginv)
        z = _idft(f1inv, zf, z, gate, hy_bias[o][None])
    return z


def kernel(x, meta_tokens, norm_mix, norm_ffn, norm_final, a_w_in, mla_q_norm, mla_w_uq, mla_kv_norm, mla_w_ukv, nat_rpb, a_w_out, c_w_in, hy_short_w, hy_short_b, hy_ffn_w1, hy_ffn_b1, hy_ffn_w2, hy_ffn_b2, hy_ffn_w3, hy_sin_freq, hy_log_decay, hy_bias, cf_dw_w, cf_dw_b, cf_ln_g, cf_ln_b, c_w_out, moe_w_group, moe_b_group, moe_w_router, moe_b_router, moe_w1, moe_w3, moe_w2):
    h = jnp.concatenate([meta_tokens, x[0], jnp.zeros((TP - T, D_MODEL), f32)], axis=0)

    def heads(a, d):
        return a.reshape(TP, HEADS, d).transpose(1, 0, 2)

    def merge(a):
        return a.transpose(1, 0, 2).reshape(TP, HEADS * 64)

    w0 = _prep_in0(a_w_in[0], mla_q_norm[0], mla_w_uq[0], mla_kv_norm[0], mla_w_ukv[0])
    q, k, v, nq, nk, nv = _in0(h, norm_mix[0][None], w0)
    mla = _mla(heads(q, 128), heads(k, 128), heads(v, 64))
    nat = _nat(heads(nq, 64), heads(nk, 64), heads(nv, 64), _nat_bias(nat_rpb[0]))
    wo = a_w_out[0].astype(bf16)
    wr, br, w13, w2h, ex = _prep_moe(moe_w_group[0], moe_b_group[0], moe_w_router[0], moe_b_router[0],
                                     moe_w1[0], moe_w3[0], moe_w2[0])
    h, xn, gate = _out_router(h, merge(mla), merge(nat), wo[:512], wo[512:], norm_ffn[0][None], wr, br)
    h = _moe(xn, gate, h, w13, w2h, ex)

    hy_in, cf_in = _in1(h, norm_mix[1][None], c_w_in[0].astype(bf16))
    z = _hyena(hy_in, hy_short_w[0], hy_short_b[0], hy_ffn_w1[0], hy_ffn_b1[0], hy_ffn_w2[0], hy_ffn_b2[0],
               hy_ffn_w3[0], hy_sin_freq[0], hy_log_decay[0], hy_bias[0])
    c = _conf(cf_in, cf_dw_w[0], cf_dw_b[0][None], cf_ln_g[0][None], cf_ln_b[0][None])
    wo = c_w_out[0].astype(bf16)
    wr, br, w13, w2h, ex = _prep_moe(moe_w_group[1], moe_b_group[1], moe_w_router[1], moe_b_router[1],
                                     moe_w1[1], moe_w3[1], moe_w2[1])
    h, xn, gate = _out_router(h, z, c, wo[:512], wo[512:], norm_ffn[1][None], wr, br)
    h = _moe(xn, gate, h, w13, w2h, ex)

    out = _final(h, norm_final[None])
    return out[N_META:T][None]
```

```python
import functools
import math

import numpy as np
import jax
import jax.numpy as jnp
from jax import lax
from jax.experimental import pallas as pl
from jax.experimental.pallas import tpu as pltpu

f32 = jnp.float32
bf16 = jnp.bfloat16

D_MODEL = 1024
SEQ = 16384
N_META = 16
T = SEQ + N_META
TM = 512
TP = 33 * TM
NB = TP // TM
EPS = 1e-6
HEADS = 8
MLA_DK = 96
GRID_W = 64
ROWS = SEQ // GRID_W
WIN_ROWS = 8
WIN_COLS = 16
NEG = -1e30
VT_ROWS = 80
NAT_UNROLL = 8

HY_W = 512
CF_K = 31
N2 = 128
N1 = 286
NFFT = N1 * N2
K1 = N1 // 2 + 1
N1P = 144
THY = N1P * N2

VMEM_LIMIT = 56 * 1024 * 1024


def _cp(sem, vmem=VMEM_LIMIT):
    return pltpu.CompilerParams(dimension_semantics=sem, vmem_limit_bytes=vmem)


def _rms(x, g):
    return x * lax.rsqrt(jnp.mean(x * x, axis=-1, keepdims=True) + EPS) * g


def _dot(a, b):
    return jnp.dot(a, b, preferred_element_type=f32)


def _dot_hi(a, b):
    return jnp.dot(a, b, preferred_element_type=f32, precision=lax.Precision.HIGHEST)


def _k_in0(h_ref, g_ref, win_ref, qn_ref, kvn_ref, wq_ref, wqr_ref, wk_ref, e_ref, er_ref, wv_ref, cos_ref, sin_ref,
           qt_out, k_out, vt_out, nq_out, nk_out, nv_out):
    i = pl.program_id(0)
    hn = _rms(h_ref[...], g_ref[...]).astype(bf16)
    p = _dot(hn, win_ref[...])
    for hd in range(HEADS):
        nq_out[hd] = p[:, 512 + 64 * hd:576 + 64 * hd].astype(bf16)
        nk_out[hd] = p[:, 1024 + 64 * hd:1088 + 64 * hd].astype(bf16)
        nv_out[hd] = p[:, 1536 + 64 * hd:1600 + 64 * hd].astype(bf16)
    cqn = _rms(p[:, 0:256], qn_ref[...]).astype(bf16)
    ckvn = _rms(p[:, 256:384], kvn_ref[...]).astype(bf16)
    kpe = p[:, 384:512].astype(bf16)
    qa = _dot(cqn, wq_ref[...])
    qr = _dot(cqn, wqr_ref[...])
    ka = _dot(ckvn, wk_ref[...]) + _dot(kpe, e_ref[...])
    kr = _dot(kpe, er_ref[...])
    va = _dot(ckvn, wv_ref[...])
    cos = cos_ref[...]
    sin = sin_ref[...]
    t = i * TM + lax.broadcasted_iota(jnp.int32, (TM, 1), 0)
    lane = lax.broadcasted_iota(jnp.int32, (1, 128), 1)
    ones_col = jnp.where((t < T) & (lane == 64), 1.0, 0.0)
    for hd in range(HEADS):
        sl = slice(hd * 128, (hd + 1) * 128)
        qt_out[sl, :] = (qa[:, sl] * cos + qr[:, sl] * sin).T.astype(bf16)
        k_out[hd] = (ka[:, sl] * cos + kr[:, sl] * sin).astype(bf16)
        vt_out[hd, 0] = (va[:, sl] + ones_col).T[:VT_ROWS].astype(bf16)


def _in0(h, g, w):
    full = lambda shape: pl.BlockSpec(shape, lambda i: (0,) * len(shape))
    row = lambda n: pl.BlockSpec((TM, n), lambda i: (i, 0))
    hm = lambda n: pl.BlockSpec((HEADS, TM, n), lambda i: (0, i, 0))
    outs = [jax.ShapeDtypeStruct((HEADS * 128, TP), bf16), jax.ShapeDtypeStruct((HEADS, TP, 128), bf16),
            jax.ShapeDtypeStruct((HEADS, NB, VT_ROWS, TM), bf16)] + [jax.ShapeDtypeStruct((HEADS, TP, 64), bf16)] * 3
    return pl.pallas_call(
        _k_in0,
        grid=(NB,),
        in_specs=[row(1024), full((1, 1024)), full((1024, 2048)), full((1, 256)), full((1, 128)),
                  full((256, 1024)), full((256, 1024)), full((128, 1024)), full((128, 1024)), full((128, 1024)),
                  full((128, 1024)), row(128), row(128)],
        out_specs=[pl.BlockSpec((HEADS * 128, TM), lambda i: (0, i)), hm(128),
                   pl.BlockSpec((HEADS, 1, VT_ROWS, TM), lambda i: (0, i, 0, 0)), hm(64), hm(64), hm(64)],
        out_shape=outs,
        compiler_params=_cp(("parallel",)),
        name="in0",
    )(h, g, w["win"], w["qn"], w["kvn"], w["wq"], w["wqr"], w["wk"], w["e"], w["er"], w["wv"], w["cos"], w["sin"])


def _k_mla(qt_ref, k_ref, vt_ref, o_ref, sa_ref, sb_ref, m_ref, acc_ref):
    qt = qt_ref[...]
    m_ref[...] = jnp.full((1, TM), NEG, f32)
    acc_ref[...] = jnp.zeros((VT_ROWS, TM), f32)

    def scores(j):
        return _dot(k_ref[0, j], qt)

    def update(s_ref, j, masked):
        s = s_ref[...]
        if masked:
            key = lax.broadcasted_iota(jnp.int32, (TM, 1), 0)
            s = jnp.where(key < T - (NB - 1) * TM, s, NEG)
        m = m_ref[...]
        m_new = jnp.maximum(m, s.max(axis=0, keepdims=True))
        p = jnp.exp2(s - m_new).astype(bf16)
        acc_ref[...] = jnp.exp2(m - m_new) * acc_ref[...] + _dot(vt_ref[0, j], p)
        m_ref[...] = m_new

    sa_ref[...] = scores(0)

    def pair(i, _):
        sb_ref[...] = scores(2 * i + 1)
        update(sa_ref, 2 * i, False)
        sa_ref[...] = scores(2 * i + 2)
        update(sb_ref, 2 * i + 1, False)
        return 0

    lax.fori_loop(0, (NB - 1) // 2, pair, 0)
    update(sa_ref, NB - 1, True)
    acc = acc_ref[...]
    o_ref[0] = (acc[:64] / acc[64:65]).astype(bf16)


def _mla(qt, k, vt):
    return pl.pallas_call(
        _k_mla,
        grid=(HEADS, NB),
        in_specs=[pl.BlockSpec((128, TM), lambda h, i: (h, i)),
                  pl.BlockSpec((1, NB, TM, 128), lambda h, i: (h, 0, 0, 0)),
                  pl.BlockSpec((1, NB, VT_ROWS, TM), lambda h, i: (h, 0, 0, 0))],
        out_specs=pl.BlockSpec((1, 64, TM), lambda h, i: (h, 0, i)),
        out_shape=jax.ShapeDtypeStruct((HEADS, 64, TP), bf16),
        scratch_shapes=[pltpu.VMEM((TM, TM), f32), pltpu.VMEM((TM, TM), f32), pltpu.VMEM((1, TM), f32),
                        pltpu.VMEM((VT_ROWS, TM), f32)],
        compiler_params=_cp(("parallel", "parallel")),
        name="mla",
    )(qt, k.reshape(HEADS, NB, TM, 128), vt)


def _k_nat(q_ref, k_ref, v_ref, c_ref, o_ref):
    km = k_ref[0, 0:N_META, :]
    vm = v_ref[0, 0:N_META, :]
    qm = q_ref[0, 0:N_META, :]
    s = lax.dot_general(qm, km, (((1,), (1,)), ((), ())), preferred_element_type=f32)
    p = jnp.exp(s - s.max(axis=1, keepdims=True))
    o_ref[0, 0:N_META, :] = (_dot(p.astype(bf16), vm) / p.sum(axis=1, keepdims=True)).astype(bf16)
    o_ref[0, T:TP, :] = jnp.zeros((TP - T, 64), bf16)

    def rows(ii, _):
        idx = [ii * NAT_UNROLL + r for r in range(NAT_UNROLL)]
        r0s = [jnp.clip(i - WIN_ROWS // 2, 0, ROWS - WIN_ROWS) for i in idx]
        qoffs = [pl.multiple_of(N_META + GRID_W * i, 16) for i in idx]
        koffs = [pl.multiple_of(N_META + GRID_W * r0, 16) for r0 in r0s]
        ss, sms = [], []
        for i, r0, qoff, koff in zip(idx, r0s, qoffs, koffs):
            a0 = r0 - i + WIN_ROWS - 1
            qi = q_ref[0, pl.ds(qoff, GRID_W), :]
            kw = k_ref[0, pl.ds(koff, WIN_ROWS * GRID_W), :]
            s = lax.dot_general(qi, kw, (((1,), (1,)), ((), ())), preferred_element_type=f32)
            ss.append(s + jnp.concatenate([c_ref[0, a0 + 2 * j] for j in range(WIN_ROWS // 2)], axis=1))
            sms.append(lax.dot_general(qi, km, (((1,), (1,)), ((), ())), preferred_element_type=f32))
        ps, pms, ls = [], [], []
        for s, sm in zip(ss, sms):
            m = jnp.maximum(s.max(axis=1, keepdims=True), sm.max(axis=1, keepdims=True))
            p = jnp.exp(s - m)
            pm = jnp.exp(sm - m)
            ls.append(p.sum(axis=1, keepdims=True) + pm.sum(axis=1, keepdims=True))
            ps.append(p.astype(bf16))
            pms.append(pm.astype(bf16))
        for p, pm, l, qoff, koff in zip(ps, pms, ls, qoffs, koffs):
            vw = v_ref[0, pl.ds(koff, WIN_ROWS * GRID_W), :]
            o = _dot(p, vw) + _dot(pm, vm)
            o_ref[0, pl.ds(qoff, GRID_W), :] = (o / l).astype(bf16)
        return 0

    lax.fori_loop(0, ROWS // NAT_UNROLL, rows, 0)


def _nat(q, k, v, c2):
    blk = pl.BlockSpec((1, TP, 64), lambda h: (h, 0, 0))
    return pl.pallas_call(
        _k_nat,
        grid=(HEADS,),
        in_specs=[blk, blk, blk, pl.BlockSpec((1, 2 * WIN_ROWS - 2, GRID_W, 2 * GRID_W), lambda h: (h, 0, 0, 0))],
        out_specs=blk,
        out_shape=jax.ShapeDtypeStruct((HEADS, TP, 64), bf16),
        compiler_params=_cp(("parallel",)),
        name="nat",
    )(q, k, v, c2)


def _k_out_router(h_ref, a_ref, b_ref, wa_ref, wb_ref, g_ref, wr_ref, br_ref, h_out, xn_out, gate_out, *, a_transposed):
    if a_transposed:
        ya = lax.dot_general(a_ref[...], wa_ref[...], (((0,), (0,)), ((), ())), preferred_element_type=f32)
    else:
        ya = _dot(a_ref[...].astype(bf16), wa_ref[...])
    h1 = h_ref[...] + ya + _dot(b_ref[...].astype(bf16), wb_ref[...])
    h_out[...] = h1
    xn = _rms(h1, g_ref[...])
    xn_out[...] = xn.astype(bf16)
    logits = _dot_hi(xn, wr_ref[...]) + br_ref[...]
    lane = lax.broadcasted_iota(jnp.int32, logits.shape, 1)
    big = jnp.int32(1 << 20)
    gl = jnp.where((lane >= 32) & (lane < 36), logits, NEG)
    gmax = gl.max(axis=1, keepdims=True)
    gsum = jnp.exp(gl - gmax).sum(axis=1, keepdims=True)
    p_group = 1.0 / gsum
    gidx = jnp.where(gl == gmax, lane, big).min(axis=1, keepdims=True) - 32
    el = jnp.where((lane >> 3) == gidx, logits, NEG)
    v1 = el.max(axis=1, keepdims=True)
    i1 = jnp.where(el == v1, lane, big).min(axis=1, keepdims=True)
    el2 = jnp.where(lane == i1, NEG, el)
    v2 = el2.max(axis=1, keepdims=True)
    i2 = jnp.where(el2 == v2, lane, big).min(axis=1, keepdims=True)
    e2 = jnp.exp(v2 - v1)
    w1 = p_group / (1.0 + e2)
    w2 = w1 * e2
    gate_out[...] = jnp.where(lane == i1, w1, jnp.where(lane == i2, w2, 0.0))


def _out_router(h, a, b, wa, wb, g, wr, br, a_transposed=False):
    full = lambda shape: pl.BlockSpec(shape, lambda i: (0,) * len(shape))
    row = lambda n: pl.BlockSpec((TM, n), lambda i: (i, 0))
    a_spec = pl.BlockSpec((512, TM), lambda i: (0, i)) if a_transposed else row(512)
    return pl.pallas_call(
        functools.partial(_k_out_router, a_transposed=a_transposed),
        grid=(NB,),
        in_specs=[row(1024), a_spec, row(512), full((512, 1024)), full((512, 1024)), full((1, 1024)),
                  full((1024, 128)), full((1, 128))],
        out_specs=[row(1024), row(1024), row(128)],
        out_shape=[jax.ShapeDtypeStruct((TP, 1024), f32), jax.ShapeDtypeStruct((TP, 1024), bf16),
                   jax.ShapeDtypeStruct((TP, 128), f32)],
        compiler_params=_cp(("parallel",)),
        name="out_router",
    )(h, a, b, wa, wb, g, wr, br)


def _k_moe(xn_ref, gate_ref, h_ref, w13_ref, w2_ref, ex_ref, o_ref):
    first = (pl.program_id(1) == 0) & (pl.program_id(2) == 0)

    @pl.when(first)
    def _():
        o_ref[...] = h_ref[...]

    a = _dot(xn_ref[...], w13_ref[0, 0])
    gate = gate_ref[...]
    ghi = gate.astype(bf16)
    glo = (gate - ghi.astype(f32)).astype(bf16)
    gx = _dot(ghi, ex_ref[0, 0]) + _dot(glo, ex_ref[0, 0])
    a1 = a[:, :1024]
    hid = a1 * jax.nn.sigmoid(a1) * a[:, 1024:] * gx
    o_ref[...] += _dot(hid.astype(bf16), w2_ref[0, 0])


def _moe(xn, gate, h, w13, w2, ex):
    row = lambda n: pl.BlockSpec((TM, n), lambda i, g, s: (i, 0))
    return pl.pallas_call(
        _k_moe,
        grid=(NB, 4, 2),
        in_specs=[row(1024), row(128), row(1024),
                  pl.BlockSpec((1, 1, 1024, 2048), lambda i, g, s: (g, s, 0, 0)),
                  pl.BlockSpec((1, 1, 1024, 1024), lambda i, g, s: (g, s, 0, 0)),
                  pl.BlockSpec((1, 1, 128, 1024), lambda i, g, s: (g, s, 0, 0))],
        out_specs=row(1024),
        out_shape=jax.ShapeDtypeStruct((TP, 1024), f32),
        compiler_params=_cp(("parallel", "arbitrary", "arbitrary")),
        name="moe",
    )(xn, gate, h, w13, w2, ex)


def _k_in1(h_ref, g_ref, w_ref, hy_out, cf_out):
    hn = _rms(h_ref[...], g_ref[...]).astype(bf16)
    p = _dot(hn, w_ref[...])
    hy_out[...] = p[:, :1536]
    cf_out[...] = p[:, 1536:]


def _in1(h, g, w):
    row = lambda n: pl.BlockSpec((TM, n), lambda i: (i, 0))
    return pl.pallas_call(
        _k_in1,
        grid=(NB,),
        in_specs=[row(1024), pl.BlockSpec((1, 1024), lambda i: (0, 0)), pl.BlockSpec((1024, 2560), lambda i: (0, 0))],
        out_specs=[row(1536), row(1024)],
        out_shape=[jax.ShapeDtypeStruct((TP, 1536), f32), jax.ShapeDtypeStruct((TP, 1024), f32)],
        compiler_params=_cp(("parallel",)),
        name="in1",
    )(h, g, w)


def _k_short(prev_ref, cur_ref, next_ref, w_ref, b_ref, v_out, x1_out, x2_out):
    i = pl.program_id(0)
    ext = jnp.concatenate([prev_ref[...], cur_ref[...], next_ref[...]], axis=0)
    t = i * TM - 8 + lax.broadcasted_iota(jnp.int32, (TM + 16, 1), 0)
    ext = jnp.where((t >= 0) & (t < T), ext, 0.0)
    w = w_ref[...]
    s = ext[7:7 + TM] * w[0:1] + ext[8:8 + TM] * w[1:2] + ext[9:9 + TM] * w[2:3] + b_ref[...]
    s = jnp.where(t[8:8 + TM] < T, s, 0.0)
    v_out[...] = s[:, :512]
    x1_out[...] = s[:, 512:1024]
    x2_out[...] = s[:, 1024:]


def _short(hy, w, b):
    last = NB - 1
    nblk = THY // TM
    return pl.pallas_call(
        _k_short,
        grid=(nblk,),
        in_specs=[pl.BlockSpec((8, 1536), lambda i: (jnp.maximum(jnp.minimum(i, last) * (TM // 8) - 1, 0), 0)),
                  pl.BlockSpec((TM, 1536), lambda i: (jnp.minimum(i, last), 0)),
                  pl.BlockSpec((8, 1536), lambda i: (jnp.minimum((jnp.minimum(i, last) + 1) * (TM // 8), TP // 8 - 1), 0)),
                  pl.BlockSpec((3, 1536), lambda i: (0, 0)), pl.BlockSpec((1, 1536), lambda i: (0, 0))],
        out_specs=[pl.BlockSpec((TM, 512), lambda i: (i, 0))] * 3,
        out_shape=[jax.ShapeDtypeStruct((THY, 512), f32)] * 3,
        compiler_params=_cp(("parallel",)),
        name="hy_short",
    )(hy, hy, hy, w, b)


def _k_filt(feat_ref, w1_ref, b1_ref, w2_ref, b2_ref, w3_ref, fr_ref, rate_ref, o_ref):
    i = pl.program_id(0)
    feats = feat_ref[...]
    fr = fr_ref[...]
    hid = jnp.sin(fr * (_dot_hi(feats, w1_ref[...]) + b1_ref[...]))
    hid = jnp.sin(fr * (_dot_hi(hid, w2_ref[...]) + b2_ref[...]))
    filt = _dot_hi(hid, w3_ref[...])
    filt = filt * jnp.exp(-feats[:, 0:1] * rate_ref[...])
    lag = i * TM + lax.broadcasted_iota(jnp.int32, (TM, 1), 0)
    col = lax.broadcasted_iota(jnp.int32, (1, 2048), 1)
    keep = (lag < T) & ((lag > 0) | (col < 1024))
    o_ref[...] = jnp.where(keep, filt, 0.0)


def _filt(feats, w1, b1, w2, b2, w3, fr, rate):
    full = lambda shape: pl.BlockSpec(shape, lambda i: (0,) * len(shape))
    return pl.pallas_call(
        _k_filt,
        grid=(THY // TM,),
        in_specs=[pl.BlockSpec((TM, 128), lambda i: (i, 0)), full((128, 64)), full((1, 64)), full((64, 64)),
                  full((1, 64)), full((64, 2048)), full((1, 64)), full((1, 2048))],
        out_specs=pl.BlockSpec((TM, 2048), lambda i: (i, 0)),
        out_shape=jax.ShapeDtypeStruct((THY, 2048), f32),
        compiler_params=_cp(("parallel",)),
        name="hy_filt",
    )(feats, w1, b1, w2, b2, w3, fr, rate)


def _k_dft1(f_ref, x_ref, y_ref):
    f = f_ref[...]
    for j in range(8):
        y = _dot(f, x_ref[:, j, :].astype(bf16))
        y_ref[:, 0, j, :] = y[:K1]
        y_ref[:, 1, j, :] = y[K1:]


def _dft1(f1c, x):
    c = x.shape[1]
    x3 = x.reshape(N1P, N2, c)
    return pl.pallas_call(
        _k_dft1,
        grid=(N2 // 8, c // 512),
        in_specs=[pl.BlockSpec((2 * K1, N1P), lambda i, cb: (0, 0)),
                  pl.BlockSpec((N1P, 8, 512), lambda i, cb: (0, i, cb))],
        out_specs=pl.BlockSpec((K1, 2, 8, 512), lambda i, cb: (0, 0, i, cb)),
        out_shape=jax.ShapeDtypeStruct((K1, 2, N2, c), f32),
        compiler_params=_cp(("parallel", "parallel")),
        name="dft1",
    )(f1c, x3)


def _k_spec_filt(y_ref, g_ref, o_ref):
    y = y_ref[0].reshape(2 * N2, 2048).astype(bf16)
    x = _dot(g_ref[0], y)
    o_ref[0, 0] = x[:N2, :1024] + x[:N2, 1024:]
    o_ref[0, 1] = x[N2:, :1024] - x[N2:, 1024:]


def _spec_filt(yf, gblk):
    return pl.pallas_call(
        _k_spec_filt,
        grid=(K1,),
        in_specs=[pl.BlockSpec((1, 2, N2, 2048), lambda k: (k, 0, 0, 0)),
                  pl.BlockSpec((1, 2 * N2, 2 * N2), lambda k: (k, 0, 0))],
        out_specs=pl.BlockSpec((1, 2, N2, 1024), lambda k: (k, 0, 0, 0)),
        out_shape=jax.ShapeDtypeStruct((K1, 2, N2, 1024), f32),
        compiler_params=_cp(("parallel",)),
        name="spec_filt",
    )(yf, gblk)


def _k_spec_conv(y_ref, kf_ref, g_ref, gi_ref, z_ref):
    y = y_ref[0].reshape(2 * N2, 512).astype(bf16)
    x = _dot(g_ref[0], y)
    xr, xi = x[:N2], x[N2:]
    kr, ki = kf_ref[0, 0], kf_ref[0, 1]
    p = jnp.concatenate([xr * kr - xi * ki, xr * ki + xi * kr], axis=0).astype(bf16)
    z_ref[0] = _dot(gi_ref[0], p).reshape(2, N2, 512)


def _spec_conv(y, kf, order, gblk, ginv):
    return pl.pallas_call(
        _k_spec_conv,
        grid=(K1,),
        in_specs=[pl.BlockSpec((1, 2, N2, 512), lambda k: (k, 0, 0, 0)),
                  pl.BlockSpec((1, 2, N2, 512), lambda k: (k, 0, 0, order)),
                  pl.BlockSpec((1, 2 * N2, 2 * N2), lambda k: (k, 0, 0)),
                  pl.BlockSpec((1, 2 * N2, 2 * N2), lambda k: (k, 0, 0))],
        out_specs=pl.BlockSpec((1, 2, N2, 512), lambda k: (k, 0, 0, 0)),
        out_shape=jax.ShapeDtypeStruct((K1, 2, N2, 512), f32),
        compiler_params=_cp(("parallel",)),
        name="spec_conv",
    )(y, kf, gblk, ginv)


def _k_idft(f_ref, z_ref, zin_ref, gate_ref, bias_ref, o_ref):
    f = f_ref[...]
    bias = bias_ref[...]
    for j in range(8):
        zj = jnp.concatenate([z_ref[:, 0, j, :], z_ref[:, 1, j, :]], axis=0).astype(bf16)
        y = _dot(f, zj)
        o_ref[:, j, :] = gate_ref[:, j, :] * (y + zin_ref[:, j, :] * bias)


def _idft(f1inv, z, zin, gate, bias):
    blk = pl.BlockSpec((N1P, 8, 512), lambda i: (0, i, 0))
    out = pl.pallas_call(
        _k_idft,
        grid=(N2 // 8,),
        in_specs=[pl.BlockSpec((N1P, 2 * K1), lambda i: (0, 0)),
                  pl.BlockSpec((K1, 2, 8, 512), lambda i: (0, 0, i, 0)), blk, blk,
                  pl.BlockSpec((1, 512), lambda i: (0, 0))],
        out_specs=blk,
        out_shape=jax.ShapeDtypeStruct((N1P, N2, 512), f32),
        compiler_params=_cp(("parallel",)),
        name="idft",
    )(f1inv, z, zin.reshape(N1P, N2, 512), gate.reshape(N1P, N2, 512), bias)
    return out.reshape(THY, 512)


def _k_conf(prev_ref, cur_ref, next_ref, w_ref, b_ref, lg_ref, lb_ref, o_ref):
    i = pl.program_id(0)
    ext = jnp.concatenate([prev_ref[...], cur_ref[...], next_ref[...]], axis=0)
    t = i * TM - 16 + lax.broadcasted_iota(jnp.int32, (TM + 32, 1), 0)
    u = ext[:, :512] * jax.nn.sigmoid(ext[:, 512:])
    u = jnp.where((t >= 0) & (t < T), u, 0.0)
    w = w_ref[...]
    acc = jnp.zeros((TM, 512), f32) + b_ref[...]
    for j in range(CF_K):
        acc = acc + u[j + 1:j + 1 + TM] * w[j:j + 1]
    mu = jnp.mean(acc, axis=-1, keepdims=True)
    xc = acc - mu
    y = xc * lax.rsqrt(jnp.mean(xc * xc, axis=-1, keepdims=True) + EPS) * lg_ref[...] + lb_ref[...]
    o_ref[...] = (y * jax.nn.sigmoid(y)).astype(bf16)


def _conf(cf, w, b, lg, lb):
    full = lambda shape: pl.BlockSpec(shape, lambda i: (0,) * len(shape))
    return pl.pallas_call(
        _k_conf,
        grid=(NB,),
        in_specs=[pl.BlockSpec((16, 1024), lambda i: (jnp.maximum(i * (TM // 16) - 1, 0), 0)),
                  pl.BlockSpec((TM, 1024), lambda i: (i, 0)),
                  pl.BlockSpec((16, 1024), lambda i: (jnp.minimum((i + 1) * (TM // 16), TP // 16 - 1), 0)),
                  full((CF_K, 512)), full((1, 512)), full((1, 512)), full((1, 512))],
        out_specs=pl.BlockSpec((TM, 512), lambda i: (i, 0)),
        out_shape=jax.ShapeDtypeStruct((TP, 512), bf16),
        compiler_params=_cp(("parallel",)),
        name="conformer",
    )(cf, cf, cf, w, b, lg, lb)


def _k_final(h_ref, g_ref, o_ref):
    o_ref[...] = _rms(h_ref[...], g_ref[...])


def _final(h, g):
    return pl.pallas_call(
        _k_final,
        grid=(NB,),
        in_specs=[pl.BlockSpec((TM, 1024), lambda i: (i, 0)), pl.BlockSpec((1, 1024), lambda i: (0, 0))],
        out_specs=pl.BlockSpec((TM, 1024), lambda i: (i, 0)),
        out_shape=jax.ShapeDtypeStruct((TP, 1024), f32),
        compiler_params=_cp(("parallel",)),
        name="final_norm",
    )(h, g)


def _prep_in0(a_w_in, q_norm, w_uq, kv_norm, w_ukv):
    win = jnp.concatenate([a_w_in[:, :416], jnp.zeros((D_MODEL, 96), f32), a_w_in[:, 416:928] * (64 ** -0.5),
                           a_w_in[:, 928:]], axis=1).astype(bf16)
    wq3 = (w_uq * (MLA_DK ** -0.5 * math.log2(math.e))).reshape(256, HEADS, MLA_DK)
    rope = wq3[..., 64:]
    rot = jnp.concatenate([-rope[..., 16:], rope[..., :16]], axis=-1)
    z64 = jnp.zeros((256, HEADS, 64), f32)
    z32 = jnp.zeros((256, HEADS, 32), f32)
    wq = jnp.concatenate([wq3, z32], axis=-1).reshape(256, 1024).astype(bf16)
    wqr = jnp.concatenate([z64, rot, z32], axis=-1).reshape(256, 1024).astype(bf16)
    kv3 = w_ukv.reshape(128, HEADS, 128)
    wk = jnp.concatenate([kv3[..., :64], jnp.zeros((128, HEADS, 64), f32)], axis=-1).reshape(128, 1024).astype(bf16)
    wv = jnp.concatenate([kv3[..., 64:], jnp.zeros((128, HEADS, 64), f32)], axis=-1).reshape(128, 1024).astype(bf16)
    e = np.zeros((128, HEADS, 128), np.float32)
    er = np.zeros((128, HEADS, 128), np.float32)
    for r in range(32):
        e[r, :, 64 + r] = 1.0
        if r < 16:
            er[r + 16, :, 64 + r] = -1.0
        else:
            er[r - 16, :, 64 + r] = 1.0
    pos = jnp.arange(TP, dtype=f32)
    inv_freq = 10000.0 ** (-jnp.arange(0, 32, 2, dtype=f32) / 32)
    ang = pos[:, None] * inv_freq[None, :]
    c, s = jnp.cos(ang), jnp.sin(ang)
    cos = jnp.concatenate([jnp.ones((TP, 64), f32), c, c, jnp.zeros((TP, 32), f32)], axis=1)
    sin = jnp.concatenate([jnp.zeros((TP, 64), f32), s, s, jnp.zeros((TP, 32), f32)], axis=1)
    return dict(win=win, qn=q_norm[None], kvn=kv_norm[None], wq=wq, wqr=wqr, wk=wk,
                e=jnp.asarray(e.reshape(128, 1024)).astype(bf16), er=jnp.asarray(er.reshape(128, 1024)).astype(bf16),
                wv=wv, cos=cos, sin=sin)


def _nat_bias(rpb):
    qc = np.arange(GRID_W)[:, None]
    kc = np.arange(GRID_W)[None, :]
    c0 = np.clip(qc - WIN_COLS // 2, 0, GRID_W - WIN_COLS)
    inside = (kc >= c0) & (kc < c0 + WIN_COLS)
    idx = np.clip(kc - qc + WIN_COLS - 1, 0, 2 * WIN_COLS - 2)
    c = jnp.where(jnp.asarray(inside)[None, None], rpb[:, :, idx], NEG)
    return jnp.concatenate([c[:, :-1], c[:, 1:]], axis=-1)


def _prep_moe(w_group, b_group, w_router, b_router, w1, w3, w2):
    wr = jnp.concatenate([w_router.transpose(1, 0, 2).reshape(D_MODEL, 32), w_group,
                          jnp.zeros((D_MODEL, 92), f32)], axis=1)
    br = jnp.concatenate([b_router.reshape(32), b_group, jnp.zeros((92,), f32)])[None]

    def half(w):
        return w.reshape(4, 2, 4, D_MODEL, 256).transpose(0, 1, 3, 2, 4).reshape(4, 2, D_MODEL, 1024)

    w13 = jnp.concatenate([half(w1), half(w3)], axis=-1).astype(bf16)
    w2h = w2.reshape(4, 2, 1024, D_MODEL).astype(bf16)
    ex = np.zeros((4, 2, 128, 4, 256), np.float32)
    for g in range(4):
        for s in range(2):
            for e in range(4):
                ex[g, s, g * 8 + s * 4 + e, e, :] = 1.0
    return wr, br, w13, w2h, jnp.asarray(ex.reshape(4, 2, 128, 1024)).astype(bf16)


def _dft_tables():
    two_pi = 2.0 * math.pi
    k1 = jnp.arange(K1, dtype=jnp.int32)
    n1 = jnp.arange(N1P, dtype=jnp.int32)
    th = two_pi * ((k1[:, None] * n1[None, :]) % N1).astype(f32) / N1
    f1c = jnp.concatenate([jnp.cos(th), -jnp.sin(th)], axis=0).astype(bf16)
    wgt = jnp.where((k1 == 0) | (k1 == N1 // 2), 1.0, 2.0).astype(f32) / NFFT
    f1inv = jnp.concatenate([jnp.cos(th).T * wgt[None, :], -jnp.sin(th).T * wgt[None, :]], axis=1).astype(bf16)
    k2 = jnp.arange(N2, dtype=jnp.int32)
    n2 = jnp.arange(N2, dtype=jnp.int32)
    k = k1[:, None, None] + N1 * k2[None, :, None]
    ang = -two_pi * ((k * n2[None, None, :]) % NFFT).astype(f32) / NFFT
    gre, gim = jnp.cos(ang), jnp.sin(ang)
    gblk = jnp.concatenate([jnp.concatenate([gre, -gim], axis=2), jnp.concatenate([gim, gre], axis=2)], axis=1)
    gre_t, gim_t = gre.transpose(0, 2, 1), gim.transpose(0, 2, 1)
    ginv = jnp.concatenate([jnp.concatenate([gre_t, gim_t], axis=2), jnp.concatenate([-gim_t, gre_t], axis=2)], axis=1)
    return f1c, f1inv, gblk.astype(bf16), ginv.astype(bf16)


def _hyena(hy_in, short_w, short_b, f_w1, f_b1, f_w2, f_b2, f_w3, sin_freq, log_decay, hy_bias):
    v, x1, x2 = _short(hy_in, short_w, short_b[None])
    f1c, f1inv, gblk, ginv = _dft_tables()
    tt = jnp.linspace(0.0, 1.0, T, dtype=f32)
    bands = jnp.linspace(1e-4, 15.0, 16, dtype=f32)
    ang = 2.0 * math.pi * tt[:, None] * bands[None, :]
    feats = jnp.concatenate([tt[:, None], jnp.cos(ang), jnp.sin(ang), jnp.zeros((T, 95), f32)], axis=-1)
    feats = jnp.concatenate([feats, jnp.zeros((THY - T, 128), f32)], axis=0)
    w1p = jnp.concatenate([f_w1, jnp.zeros((95, 64), f32)], axis=0)
    rate = jnp.exp(log_decay.astype(f32)).reshape(1, 2048)
    filt = _filt(feats, w1p, f_b1[None], f_w2, f_b2[None], f_w3, sin_freq[None], rate)
    kf = _spec_filt(_dft1(f1c, filt), gblk)
    z = v
    for o, gate in enumerate((x1, x2)):
        zf = _spec_conv(_dft1(f1c, z), kf, o, gblk, ginv)
        z = _idft(f1inv, zf, z, gate, hy_bias[o][None])
    return z


def kernel(x, meta_tokens, norm_mix, norm_ffn, norm_final, a_w_in, mla_q_norm, mla_w_uq, mla_kv_norm, mla_w_ukv, nat_rpb, a_w_out, c_w_in, hy_short_w, hy_short_b, hy_ffn_w1, hy_ffn_b1, hy_ffn_w2, hy_ffn_b2, hy_ffn_w3, hy_sin_freq, hy_log_decay, hy_bias, cf_dw_w, cf_dw_b, cf_ln_g, cf_ln_b, c_w_out, moe_w_group, moe_b_group, moe_w_router, moe_b_router, moe_w1, moe_w3, moe_w2):
    h = jnp.concatenate([meta_tokens, x[0], jnp.zeros((TP - T, D_MODEL), f32)], axis=0)

    w0 = _prep_in0(a_w_in[0], mla_q_norm[0], mla_w_uq[0], mla_kv_norm[0], mla_w_ukv[0])
    qt, k, vt, nq, nk, nv = _in0(h, norm_mix[0][None], w0)
    mla_t = _mla(qt, k, vt).reshape(HEADS * 64, TP)
    nat = _nat(nq, nk, nv, _nat_bias(nat_rpb[0]))
    nat = nat.transpose(1, 0, 2).reshape(TP, HEADS * 64)
    wo = a_w_out[0].astype(bf16)
    wr, br, w13, w2h, ex = _prep_moe(moe_w_group[0], moe_b_group[0], moe_w_router[0], moe_b_router[0],
                                     moe_w1[0], moe_w3[0], moe_w2[0])
    h, xn, gate = _out_router(h, mla_t, nat, wo[:512], wo[512:], norm_ffn[0][None], wr, br, a_transposed=True)
    h = _moe(xn, gate, h, w13, w2h, ex)

    hy_in, cf_in = _in1(h, norm_mix[1][None], c_w_in[0].astype(bf16))
    z = _hyena(hy_in, hy_short_w[0], hy_short_b[0], hy_ffn_w1[0], hy_ffn_b1[0], hy_ffn_w2[0], hy_ffn_b2[0],
               hy_ffn_w3[0], hy_sin_freq[0], hy_log_decay[0], hy_bias[0])
    c = _conf(cf_in, cf_dw_w[0], cf_dw_b[0][None], cf_ln_g[0][None], cf_ln_b[0][None])
    wo = c_w_out[0].astype(bf16)
    wr, br, w13, w2h, ex = _prep_moe(moe_w_group[1], moe_b_group[1], moe_w_router[1], moe_b_router[1],
                                     moe_w1[1], moe_w3[1], moe_w2[1])
    h, xn, gate = _out_router(h, z, c, wo[:512], wo[512:], norm_ffn[1][None], wr, br)
    h = _moe(xn, gate, h, w13, w2h, ex)

    out = _final(h, norm_final[None])
    return out[N_META:T][None]
```

```python
import functools
import math

import numpy as np
import jax
import jax.numpy as jnp
from jax import lax
from jax.experimental import pallas as pl
from jax.experimental.pallas import tpu as pltpu

f32 = jnp.float32
bf16 = jnp.bfloat16

D_MODEL = 1024
SEQ = 16384
N_META = 16
T = SEQ + N_META
TM = 512
TP = 33 * TM
NB = TP // TM
EPS = 1e-6
HEADS = 8
MLA_DK = 96
GRID_W = 64
ROWS = SEQ // GRID_W
WIN_ROWS = 8
WIN_COLS = 16
NEG = -1e30
VT_ROWS = 80
NAT_UNROLL = 8
MLA_UNROLL = 8

HY_W = 512
CF_K = 31
N2 = 128
N1 = 286
NFFT = N1 * N2
K1 = N1 // 2 + 1
N1P = 144
THY = N1P * N2

VMEM_LIMIT = 56 * 1024 * 1024


def _cp(sem, vmem=VMEM_LIMIT):
    return pltpu.CompilerParams(dimension_semantics=sem, vmem_limit_bytes=vmem)


def _rms(x, g):
    return x * lax.rsqrt(jnp.mean(x * x, axis=-1, keepdims=True) + EPS) * g


def _dot(a, b):
    return jnp.dot(a, b, preferred_element_type=f32)


def _dot_hi(a, b):
    return jnp.dot(a, b, preferred_element_type=f32, precision=lax.Precision.HIGHEST)


def _k_in0(h_ref, g_ref, win_ref, qn_ref, kvn_ref, wq_ref, wqr_ref, wk_ref, e_ref, er_ref, wv_ref, cos_ref, sin_ref,
           qt_out, k_out, vt_out, nq_out, nk_out, nv_out):
    i = pl.program_id(0)
    hn = _rms(h_ref[...], g_ref[...]).astype(bf16)
    p = _dot(hn, win_ref[...])
    for hd in range(HEADS):
        nq_out[hd] = p[:, 512 + 64 * hd:576 + 64 * hd].astype(bf16)
        nk_out[hd] = p[:, 1024 + 64 * hd:1088 + 64 * hd].astype(bf16)
        nv_out[hd] = p[:, 1536 + 64 * hd:1600 + 64 * hd].astype(bf16)
    cqn = _rms(p[:, 0:256], qn_ref[...]).astype(bf16)
    ckvn = _rms(p[:, 256:384], kvn_ref[...]).astype(bf16)
    kpe = p[:, 384:512].astype(bf16)
    qa = _dot(cqn, wq_ref[...])
    qr = _dot(cqn, wqr_ref[...])
    ka = _dot(ckvn, wk_ref[...]) + _dot(kpe, e_ref[...])
    kr = _dot(kpe, er_ref[...])
    va = _dot(ckvn, wv_ref[...])
    cos = cos_ref[...]
    sin = sin_ref[...]
    t = i * TM + lax.broadcasted_iota(jnp.int32, (TM, 1), 0)
    lane = lax.broadcasted_iota(jnp.int32, (1, 128), 1)
    ones_col = jnp.where((t < T) & (lane == 64), 1.0, 0.0)
    for hd in range(HEADS):
        sl = slice(hd * 128, (hd + 1) * 128)
        qt_out[sl, :] = (qa[:, sl] * cos + qr[:, sl] * sin).T.astype(bf16)
        k_out[hd] = (ka[:, sl] * cos + kr[:, sl] * sin).astype(bf16)
        vt_out[hd, 0] = (va[:, sl] + ones_col).T[:VT_ROWS].astype(bf16)


def _in0(h, g, w):
    full = lambda shape: pl.BlockSpec(shape, lambda i: (0,) * len(shape))
    row = lambda n: pl.BlockSpec((TM, n), lambda i: (i, 0))
    hm = lambda n: pl.BlockSpec((HEADS, TM, n), lambda i: (0, i, 0))
    outs = [jax.ShapeDtypeStruct((HEADS * 128, TP), bf16), jax.ShapeDtypeStruct((HEADS, TP, 128), bf16),
            jax.ShapeDtypeStruct((HEADS, NB, VT_ROWS, TM), bf16)] + [jax.ShapeDtypeStruct((HEADS, TP, 64), bf16)] * 3
    return pl.pallas_call(
        _k_in0,
        grid=(NB,),
        in_specs=[row(1024), full((1, 1024)), full((1024, 2048)), full((1, 256)), full((1, 128)),
                  full((256, 1024)), full((256, 1024)), full((128, 1024)), full((128, 1024)), full((128, 1024)),
                  full((128, 1024)), row(128), row(128)],
        out_specs=[pl.BlockSpec((HEADS * 128, TM), lambda i: (0, i)), hm(128),
                   pl.BlockSpec((HEADS, 1, VT_ROWS, TM), lambda i: (0, i, 0, 0)), hm(64), hm(64), hm(64)],
        out_shape=outs,
        compiler_params=_cp(("parallel",)),
        name="in0",
    )(h, g, w["win"], w["qn"], w["kvn"], w["wq"], w["wqr"], w["wk"], w["e"], w["er"], w["wv"], w["cos"], w["sin"])


def _k_mla(qt_ref, k_ref, vt_ref, o_ref, sa_ref, sb_ref, m_ref, acc_ref):
    qt = qt_ref[...]
    m_ref[...] = jnp.full((1, TM), NEG, f32)
    acc_ref[...] = jnp.zeros((VT_ROWS, TM), f32)

    def scores(j):
        return _dot(k_ref[0, j], qt)

    def update(s_ref, j, masked):
        s = s_ref[...]
        if masked:
            key = lax.broadcasted_iota(jnp.int32, (TM, 1), 0)
            s = jnp.where(key < T - (NB - 1) * TM, s, NEG)
        m = m_ref[...]
        m_new = jnp.maximum(m, s.max(axis=0, keepdims=True))
        p = jnp.exp2(s - m_new).astype(bf16)
        acc_ref[...] = jnp.exp2(m - m_new) * acc_ref[...] + _dot(vt_ref[0, j], p)
        m_ref[...] = m_new

    sa_ref[...] = scores(0)

    def group(i, _):
        for u in range(0, MLA_UNROLL, 2):
            j = MLA_UNROLL * i + u
            sb_ref[...] = scores(j + 1)
            update(sa_ref, j, False)
            sa_ref[...] = scores(j + 2)
            update(sb_ref, j + 1, False)
        return 0

    lax.fori_loop(0, (NB - 1) // MLA_UNROLL, group, 0)
    update(sa_ref, NB - 1, True)
    acc = acc_ref[...]
    o_ref[0] = (acc[:64] / acc[64:65]).astype(bf16)


def _mla(qt, k, vt):
    return pl.pallas_call(
        _k_mla,
        grid=(HEADS, NB),
        in_specs=[pl.BlockSpec((128, TM), lambda h, i: (h, i)),
                  pl.BlockSpec((1, NB, TM, 128), lambda h, i: (h, 0, 0, 0)),
                  pl.BlockSpec((1, NB, VT_ROWS, TM), lambda h, i: (h, 0, 0, 0))],
        out_specs=pl.BlockSpec((1, 64, TM), lambda h, i: (h, 0, i)),
        out_shape=jax.ShapeDtypeStruct((HEADS, 64, TP), bf16),
        scratch_shapes=[pltpu.VMEM((TM, TM), f32), pltpu.VMEM((TM, TM), f32), pltpu.VMEM((1, TM), f32),
                        pltpu.VMEM((VT_ROWS, TM), f32)],
        compiler_params=_cp(("parallel", "parallel")),
        name="mla",
    )(qt, k.reshape(HEADS, NB, TM, 128), vt)


def _k_nat(q_ref, k_ref, v_ref, c_ref, o_ref):
    km = k_ref[0, 0:N_META, :]
    vm = v_ref[0, 0:N_META, :]
    qm = q_ref[0, 0:N_META, :]
    s = lax.dot_general(qm, km, (((1,), (1,)), ((), ())), preferred_element_type=f32)
    p = jnp.exp(s - s.max(axis=1, keepdims=True))
    o_ref[0, 0:N_META, :] = (_dot(p.astype(bf16), vm) / p.sum(axis=1, keepdims=True)).astype(bf16)
    o_ref[0, T:TP, :] = jnp.zeros((TP - T, 64), bf16)

    def rows(ii, _):
        idx = [ii * NAT_UNROLL + r for r in range(NAT_UNROLL)]
        r0s = [jnp.clip(i - WIN_ROWS // 2, 0, ROWS - WIN_ROWS) for i in idx]
        qoffs = [pl.multiple_of(N_META + GRID_W * i, 16) for i in idx]
        koffs = [pl.multiple_of(N_META + GRID_W * r0, 16) for r0 in r0s]
        ss, sms = [], []
        for i, r0, qoff, koff in zip(idx, r0s, qoffs, koffs):
            a0 = r0 - i + WIN_ROWS - 1
            qi = q_ref[0, pl.ds(qoff, GRID_W), :]
            kw = k_ref[0, pl.ds(koff, WIN_ROWS * GRID_W), :]
            s = lax.dot_general(qi, kw, (((1,), (1,)), ((), ())), preferred_element_type=f32)
            ss.append(s + jnp.concatenate([c_ref[0, a0 + 2 * j] for j in range(WIN_ROWS // 2)], axis=1))
            sms.append(lax.dot_general(qi, km, (((1,), (1,)), ((), ())), preferred_element_type=f32))
        ps, pms, ls = [], [], []
        for s, sm in zip(ss, sms):
            m = jnp.maximum(s.max(axis=1, keepdims=True), sm.max(axis=1, keepdims=True))
            p = jnp.exp(s - m)
            pm = jnp.exp(sm - m)
            ls.append(p.sum(axis=1, keepdims=True) + pm.sum(axis=1, keepdims=True))
            ps.append(p.astype(bf16))
            pms.append(pm.astype(bf16))
        for p, pm, l, qoff, koff in zip(ps, pms, ls, qoffs, koffs):
            vw = v_ref[0, pl.ds(koff, WIN_ROWS * GRID_W), :]
            o = _dot(p, vw) + _dot(pm, vm)
            o_ref[0, pl.ds(qoff, GRID_W), :] = (o / l).astype(bf16)
        return 0

    lax.fori_loop(0, ROWS // NAT_UNROLL, rows, 0)


def _nat(q, k, v, c2):
    blk = pl.BlockSpec((1, TP, 64), lambda h: (h, 0, 0))
    return pl.pallas_call(
        _k_nat,
        grid=(HEADS,),
        in_specs=[blk, blk, blk, pl.BlockSpec((1, 2 * WIN_ROWS - 2, GRID_W, 2 * GRID_W), lambda h: (h, 0, 0, 0))],
        out_specs=blk,
        out_shape=jax.ShapeDtypeStruct((HEADS, TP, 64), bf16),
        compiler_params=_cp(("parallel",)),
        name="nat",
    )(q, k, v, c2)


def _k_out_router(h_ref, a_ref, b_ref, wa_ref, wb_ref, g_ref, wr_ref, br_ref, h_out, xn_out, gate_out, *, a_transposed):
    if a_transposed:
        ya = lax.dot_general(a_ref[...], wa_ref[...], (((0,), (0,)), ((), ())), preferred_element_type=f32)
    else:
        ya = _dot(a_ref[...].reshape(TM, 512).astype(bf16), wa_ref[...])
    h1 = h_ref[...] + ya + _dot(b_ref[...].astype(bf16), wb_ref[...])
    h_out[...] = h1
    xn = _rms(h1, g_ref[...])
    xn_out[...] = xn.astype(bf16)
    logits = _dot_hi(xn, wr_ref[...]) + br_ref[...]
    lane = lax.broadcasted_iota(jnp.int32, logits.shape, 1)
    big = jnp.int32(1 << 20)
    gl = jnp.where((lane >= 32) & (lane < 36), logits, NEG)
    gmax = gl.max(axis=1, keepdims=True)
    gsum = jnp.exp(gl - gmax).sum(axis=1, keepdims=True)
    p_group = 1.0 / gsum
    gidx = jnp.where(gl == gmax, lane, big).min(axis=1, keepdims=True) - 32
    el = jnp.where((lane >> 3) == gidx, logits, NEG)
    v1 = el.max(axis=1, keepdims=True)
    i1 = jnp.where(el == v1, lane, big).min(axis=1, keepdims=True)
    el2 = jnp.where(lane == i1, NEG, el)
    v2 = el2.max(axis=1, keepdims=True)
    i2 = jnp.where(el2 == v2, lane, big).min(axis=1, keepdims=True)
    e2 = jnp.exp(v2 - v1)
    w1 = p_group / (1.0 + e2)
    w2 = w1 * e2
    gate_out[...] = jnp.where(lane == i1, w1, jnp.where(lane == i2, w2, 0.0))


def _out_router(h, a, b, wa, wb, g, wr, br, a_transposed=False):
    full = lambda shape: pl.BlockSpec(shape, lambda i: (0,) * len(shape))
    row = lambda n: pl.BlockSpec((TM, n), lambda i: (i, 0))
    a_spec = (pl.BlockSpec((512, TM), lambda i: (0, i)) if a_transposed
              else pl.BlockSpec((TM // N2, N2, 512), lambda i: (i, 0, 0)))
    return pl.pallas_call(
        functools.partial(_k_out_router, a_transposed=a_transposed),
        grid=(NB,),
        in_specs=[row(1024), a_spec, row(512), full((512, 1024)), full((512, 1024)), full((1, 1024)),
                  full((1024, 128)), full((1, 128))],
        out_specs=[row(1024), row(1024), row(128)],
        out_shape=[jax.ShapeDtypeStruct((TP, 1024), f32), jax.ShapeDtypeStruct((TP, 1024), bf16),
                   jax.ShapeDtypeStruct((TP, 128), f32)],
        compiler_params=_cp(("parallel",)),
        name="out_router",
    )(h, a, b, wa, wb, g, wr, br)


def _k_moe(xn_ref, gate_ref, h_ref, w13_ref, w2_ref, ex_ref, o_ref):
    first = (pl.program_id(1) == 0) & (pl.program_id(2) == 0)

    @pl.when(first)
    def _():
        o_ref[...] = h_ref[...]

    a = _dot(xn_ref[...], w13_ref[0, 0])
    gate = gate_ref[...]
    ghi = gate.astype(bf16)
    glo = (gate - ghi.astype(f32)).astype(bf16)
    gx = _dot(ghi, ex_ref[0, 0]) + _dot(glo, ex_ref[0, 0])
    a1 = a[:, :1024]
    hid = a1 * jax.nn.sigmoid(a1) * a[:, 1024:] * gx
    o_ref[...] += _dot(hid.astype(bf16), w2_ref[0, 0])


def _moe(xn, gate, h, w13, w2, ex):
    row = lambda n: pl.BlockSpec((TM, n), lambda i, g, s: (i, 0))
    return pl.pallas_call(
        _k_moe,
        grid=(NB, 4, 2),
        in_specs=[row(1024), row(128), row(1024),
                  pl.BlockSpec((1, 1, 1024, 2048), lambda i, g, s: (g, s, 0, 0)),
                  pl.BlockSpec((1, 1, 1024, 1024), lambda i, g, s: (g, s, 0, 0)),
                  pl.BlockSpec((1, 1, 128, 1024), lambda i, g, s: (g, s, 0, 0))],
        out_specs=row(1024),
        out_shape=jax.ShapeDtypeStruct((TP, 1024), f32),
        compiler_params=_cp(("parallel", "arbitrary", "arbitrary")),
        name="moe",
    )(xn, gate, h, w13, w2, ex)


def _k_in1(h_ref, g_ref, w_ref, hy_out, cf_out):
    hn = _rms(h_ref[...], g_ref[...]).astype(bf16)
    p = _dot(hn, w_ref[...])
    hy_out[...] = p[:, :1536]
    cf_out[...] = p[:, 1536:]


def _in1(h, g, w):
    row = lambda n: pl.BlockSpec((TM, n), lambda i: (i, 0))
    return pl.pallas_call(
        _k_in1,
        grid=(NB,),
        in_specs=[row(1024), pl.BlockSpec((1, 1024), lambda i: (0, 0)), pl.BlockSpec((1024, 2560), lambda i: (0, 0))],
        out_specs=[row(1536), row(1024)],
        out_shape=[jax.ShapeDtypeStruct((TP, 1536), f32), jax.ShapeDtypeStruct((TP, 1024), f32)],
        compiler_params=_cp(("parallel",)),
        name="in1",
    )(h, g, w)


def _k_short(prev_ref, cur_ref, next_ref, w_ref, b_ref, v_out, x1_out, x2_out):
    i = pl.program_id(0)
    ext = jnp.concatenate([prev_ref[...], cur_ref[...], next_ref[...]], axis=0)
    t = i * TM - 8 + lax.broadcasted_iota(jnp.int32, (TM + 16, 1), 0)
    ext = jnp.where((t >= 0) & (t < T), ext, 0.0)
    w = w_ref[...]
    s = ext[7:7 + TM] * w[0:1] + ext[8:8 + TM] * w[1:2] + ext[9:9 + TM] * w[2:3] + b_ref[...]
    s = jnp.where(t[8:8 + TM] < T, s, 0.0)
    v_out[...] = s[:, :512].reshape(TM // N2, N2, 512)
    x1_out[...] = s[:, 512:1024].reshape(TM // N2, N2, 512)
    x2_out[...] = s[:, 1024:].reshape(TM // N2, N2, 512)


def _short(hy, w, b):
    last = NB - 1
    nblk = THY // TM
    return pl.pallas_call(
        _k_short,
        grid=(nblk,),
        in_specs=[pl.BlockSpec((8, 1536), lambda i: (jnp.maximum(jnp.minimum(i, last) * (TM // 8) - 1, 0), 0)),
                  pl.BlockSpec((TM, 1536), lambda i: (jnp.minimum(i, last), 0)),
                  pl.BlockSpec((8, 1536), lambda i: (jnp.minimum((jnp.minimum(i, last) + 1) * (TM // 8), TP // 8 - 1), 0)),
                  pl.BlockSpec((3, 1536), lambda i: (0, 0)), pl.BlockSpec((1, 1536), lambda i: (0, 0))],
        out_specs=[pl.BlockSpec((TM // N2, N2, 512), lambda i: (i, 0, 0))] * 3,
        out_shape=[jax.ShapeDtypeStruct((N1P, N2, 512), f32)] * 3,
        compiler_params=_cp(("parallel",)),
        name="hy_short",
    )(hy, hy, hy, w, b)


def _k_filt(feat_ref, w1_ref, b1_ref, w2_ref, b2_ref, w3_ref, fr_ref, rate_ref, o_ref):
    i = pl.program_id(0)
    feats = feat_ref[...]
    fr = fr_ref[...]
    hid = jnp.sin(fr * (_dot_hi(feats, w1_ref[...]) + b1_ref[...]))
    hid = jnp.sin(fr * (_dot_hi(hid, w2_ref[...]) + b2_ref[...]))
    filt = _dot(hid.astype(bf16), w3_ref[...].astype(bf16))
    filt = filt * jnp.exp(-feats[:, 0:1] * rate_ref[...])
    lag = i * TM + lax.broadcasted_iota(jnp.int32, (TM, 1), 0)
    col = lax.broadcasted_iota(jnp.int32, (1, 2048), 1)
    keep = (lag < T) & ((lag > 0) | (col < 1024))
    o_ref[...] = jnp.where(keep, filt, 0.0).reshape(TM // N2, N2, 2048)


def _filt(feats, w1, b1, w2, b2, w3, fr, rate):
    full = lambda shape: pl.BlockSpec(shape, lambda i: (0,) * len(shape))
    return pl.pallas_call(
        _k_filt,
        grid=(THY // TM,),
        in_specs=[pl.BlockSpec((TM, 128), lambda i: (i, 0)), full((128, 64)), full((1, 64)), full((64, 64)),
                  full((1, 64)), full((64, 2048)), full((1, 64)), full((1, 2048))],
        out_specs=pl.BlockSpec((TM // N2, N2, 2048), lambda i: (i, 0, 0)),
        out_shape=jax.ShapeDtypeStruct((N1P, N2, 2048), f32),
        compiler_params=_cp(("parallel",)),
        name="hy_filt",
    )(feats, w1, b1, w2, b2, w3, fr, rate)


def _k_dft1(f_ref, x_ref, y_ref):
    f = f_ref[...]
    for j in range(8):
        y = _dot(f, x_ref[:, j, :].astype(bf16))
        y_ref[:, 0, j, :] = y[:K1]
        y_ref[:, 1, j, :] = y[K1:]


def _dft1(f1c, x3):
    c = x3.shape[2]
    return pl.pallas_call(
        _k_dft1,
        grid=(N2 // 8, c // 512),
        in_specs=[pl.BlockSpec((2 * K1, N1P), lambda i, cb: (0, 0)),
                  pl.BlockSpec((N1P, 8, 512), lambda i, cb: (0, i, cb))],
        out_specs=pl.BlockSpec((K1, 2, 8, 512), lambda i, cb: (0, 0, i, cb)),
        out_shape=jax.ShapeDtypeStruct((K1, 2, N2, c), f32),
        compiler_params=_cp(("parallel", "parallel")),
        name="dft1",
    )(f1c, x3)


def _stage2_matrix(f2_ref, tw_ref):
    fr, fi = f2_ref[0], f2_ref[1]
    tr, ti = tw_ref[0, 0:1, :], tw_ref[0, 1:2, :]
    gre = fr * tr - fi * ti
    gim = fr * ti + fi * tr
    return jnp.concatenate([jnp.concatenate([gre, -gim], axis=1), jnp.concatenate([gim, gre], axis=1)], axis=0)


def _k_spec_filt(y_ref, f2_ref, tw_ref, o_ref):
    y = y_ref[0].reshape(2 * N2, 2048).astype(bf16)
    x = _dot(_stage2_matrix(f2_ref, tw_ref).astype(bf16), y)
    o_ref[0, 0] = x[:N2, :1024] + x[:N2, 1024:]
    o_ref[0, 1] = x[N2:, :1024] - x[N2:, 1024:]


def _spec_filt(yf, f2, tw):
    return pl.pallas_call(
        _k_spec_filt,
        grid=(K1,),
        in_specs=[pl.BlockSpec((1, 2, N2, 2048), lambda k: (k, 0, 0, 0)),
                  pl.BlockSpec((2, N2, N2), lambda k: (0, 0, 0)), pl.BlockSpec((1, 2, N2), lambda k: (k, 0, 0))],
        out_specs=pl.BlockSpec((1, 2, N2, 1024), lambda k: (k, 0, 0, 0)),
        out_shape=jax.ShapeDtypeStruct((K1, 2, N2, 1024), f32),
        compiler_params=_cp(("parallel",)),
        name="spec_filt",
    )(yf, f2, tw)


def _k_spec_conv(y_ref, kf_ref, f2_ref, tw_ref, z_ref):
    g = _stage2_matrix(f2_ref, tw_ref)
    y = y_ref[0].reshape(2 * N2, 512).astype(bf16)
    x = _dot(g.astype(bf16), y)
    xr, xi = x[:N2], x[N2:]
    kr, ki = kf_ref[0, 0], kf_ref[0, 1]
    p = jnp.concatenate([xr * kr - xi * ki, xr * ki + xi * kr], axis=0).astype(bf16)
    z_ref[0] = _dot(g.T.astype(bf16), p).reshape(2, N2, 512)


def _spec_conv(y, kf, order, f2, tw):
    return pl.pallas_call(
        _k_spec_conv,
        grid=(K1,),
        in_specs=[pl.BlockSpec((1, 2, N2, 512), lambda k: (k, 0, 0, 0)),
                  pl.BlockSpec((1, 2, N2, 512), lambda k: (k, 0, 0, order)),
                  pl.BlockSpec((2, N2, N2), lambda k: (0, 0, 0)), pl.BlockSpec((1, 2, N2), lambda k: (k, 0, 0))],
        out_specs=pl.BlockSpec((1, 2, N2, 512), lambda k: (k, 0, 0, 0)),
        out_shape=jax.ShapeDtypeStruct((K1, 2, N2, 512), f32),
        compiler_params=_cp(("parallel",)),
        name="spec_conv",
    )(y, kf, f2, tw)


def _k_idft(f_ref, z_ref, zin_ref, gate_ref, bias_ref, o_ref):
    f = f_ref[...]
    bias = bias_ref[...]
    for j in range(8):
        zj = jnp.concatenate([z_ref[:, 0, j, :], z_ref[:, 1, j, :]], axis=0).astype(bf16)
        y = _dot(f, zj)
        o_ref[:, j, :] = gate_ref[:, j, :] * (y + zin_ref[:, j, :] * bias)


def _idft(f1inv, z, zin, gate, bias):
    blk = pl.BlockSpec((N1P, 8, 512), lambda i: (0, i, 0))
    return pl.pallas_call(
        _k_idft,
        grid=(N2 // 8,),
        in_specs=[pl.BlockSpec((N1P, 2 * K1), lambda i: (0, 0)),
                  pl.BlockSpec((K1, 2, 8, 512), lambda i: (0, 0, i, 0)), blk, blk,
                  pl.BlockSpec((1, 512), lambda i: (0, 0))],
        out_specs=blk,
        out_shape=jax.ShapeDtypeStruct((N1P, N2, 512), f32),
        compiler_params=_cp(("parallel",)),
        name="idft",
    )(f1inv, z, zin, gate, bias)


def _k_conf(prev_ref, cur_ref, next_ref, w_ref, b_ref, lg_ref, lb_ref, o_ref):
    i = pl.program_id(0)
    ext = jnp.concatenate([prev_ref[...], cur_ref[...], next_ref[...]], axis=0)
    t = i * TM - 16 + lax.broadcasted_iota(jnp.int32, (TM + 32, 1), 0)
    u = ext[:, :512] * jax.nn.sigmoid(ext[:, 512:])
    u = jnp.where((t >= 0) & (t < T), u, 0.0)
    w = w_ref[...]
    acc = jnp.zeros((TM, 512), f32) + b_ref[...]
    for j in range(CF_K):
        acc = acc + u[j + 1:j + 1 + TM] * w[j:j + 1]
    mu = jnp.mean(acc, axis=-1, keepdims=True)
    xc = acc - mu
    y = xc * lax.rsqrt(jnp.mean(xc * xc, axis=-1, keepdims=True) + EPS) * lg_ref[...] + lb_ref[...]
    o_ref[...] = (y * jax.nn.sigmoid(y)).astype(bf16)


def _conf(cf, w, b, lg, lb):
    full = lambda shape: pl.BlockSpec(shape, lambda i: (0,) * len(shape))
    return pl.pallas_call(
        _k_conf,
        grid=(NB,),
        in_specs=[pl.BlockSpec((16, 1024), lambda i: (jnp.maximum(i * (TM // 16) - 1, 0), 0)),
                  pl.BlockSpec((TM, 1024), lambda i: (i, 0)),
                  pl.BlockSpec((16, 1024), lambda i: (jnp.minimum((i + 1) * (TM // 16), TP // 16 - 1), 0)),
                  full((CF_K, 512)), full((1, 512)), full((1, 512)), full((1, 512))],
        out_specs=pl.BlockSpec((TM, 512), lambda i: (i, 0)),
        out_shape=jax.ShapeDtypeStruct((TP, 512), bf16),
        compiler_params=_cp(("parallel",)),
        name="conformer",
    )(cf, cf, cf, w, b, lg, lb)


def _k_final(h_ref, nxt_ref, g_ref, o_ref):
    h = jnp.concatenate([h_ref[N_META:, :], nxt_ref[...]], axis=0)
    o_ref[...] = _rms(h, g_ref[...])


def _final(h, g):
    return pl.pallas_call(
        _k_final,
        grid=(SEQ // TM,),
        in_specs=[pl.BlockSpec((TM, 1024), lambda i: (i, 0)),
                  pl.BlockSpec((N_META, 1024), lambda i: ((i + 1) * (TM // N_META), 0)),
                  pl.BlockSpec((1, 1024), lambda i: (0, 0))],
        out_specs=pl.BlockSpec((TM, 1024), lambda i: (i, 0)),
        out_shape=jax.ShapeDtypeStruct((SEQ, 1024), f32),
        compiler_params=_cp(("parallel",)),
        name="final_norm",
    )(h, h, g)


def _prep_in0(a_w_in, q_norm, w_uq, kv_norm, w_ukv):
    win = jnp.concatenate([a_w_in[:, :416], jnp.zeros((D_MODEL, 96), f32), a_w_in[:, 416:928] * (64 ** -0.5),
                           a_w_in[:, 928:]], axis=1).astype(bf16)
    wq3 = (w_uq * (MLA_DK ** -0.5 * math.log2(math.e))).reshape(256, HEADS, MLA_DK)
    rope = wq3[..., 64:]
    rot = jnp.concatenate([-rope[..., 16:], rope[..., :16]], axis=-1)
    z64 = jnp.zeros((256, HEADS, 64), f32)
    z32 = jnp.zeros((256, HEADS, 32), f32)
    wq = jnp.concatenate([wq3, z32], axis=-1).reshape(256, 1024).astype(bf16)
    wqr = jnp.concatenate([z64, rot, z32], axis=-1).reshape(256, 1024).astype(bf16)
    kv3 = w_ukv.reshape(128, HEADS, 128)
    wk = jnp.concatenate([kv3[..., :64], jnp.zeros((128, HEADS, 64), f32)], axis=-1).reshape(128, 1024).astype(bf16)
    wv = jnp.concatenate([kv3[..., 64:], jnp.zeros((128, HEADS, 64), f32)], axis=-1).reshape(128, 1024).astype(bf16)
    e = np.zeros((128, HEADS, 128), np.float32)
    er = np.zeros((128, HEADS, 128), np.float32)
    for r in range(32):
        e[r, :, 64 + r] = 1.0
        if r < 16:
            er[r + 16, :, 64 + r] = -1.0
        else:
            er[r - 16, :, 64 + r] = 1.0
    pos = jnp.arange(TP, dtype=f32)
    inv_freq = 10000.0 ** (-jnp.arange(0, 32, 2, dtype=f32) / 32)
    ang = pos[:, None] * inv_freq[None, :]
    c, s = jnp.cos(ang), jnp.sin(ang)
    cos = jnp.concatenate([jnp.ones((TP, 64), f32), c, c, jnp.zeros((TP, 32), f32)], axis=1)
    sin = jnp.concatenate([jnp.zeros((TP, 64), f32), s, s, jnp.zeros((TP, 32), f32)], axis=1)
    return dict(win=win, qn=q_norm[None], kvn=kv_norm[None], wq=wq, wqr=wqr, wk=wk,
                e=jnp.asarray(e.reshape(128, 1024)).astype(bf16), er=jnp.asarray(er.reshape(128, 1024)).astype(bf16),
                wv=wv, cos=cos, sin=sin)


def _nat_bias(rpb):
    qc = np.arange(GRID_W)[:, None]
    kc = np.arange(GRID_W)[None, :]
    c0 = np.clip(qc - WIN_COLS // 2, 0, GRID_W - WIN_COLS)
    inside = (kc >= c0) & (kc < c0 + WIN_COLS)
    idx = np.clip(kc - qc + WIN_COLS - 1, 0, 2 * WIN_COLS - 2)
    c = jnp.where(jnp.asarray(inside)[None, None], rpb[:, :, idx], NEG)
    return jnp.concatenate([c[:, :-1], c[:, 1:]], axis=-1)


def _prep_moe(w_group, b_group, w_router, b_router, w1, w3, w2):
    wr = jnp.concatenate([w_router.transpose(1, 0, 2).reshape(D_MODEL, 32), w_group,
                          jnp.zeros((D_MODEL, 92), f32)], axis=1)
    br = jnp.concatenate([b_router.reshape(32), b_group, jnp.zeros((92,), f32)])[None]

    def half(w):
        return w.reshape(4, 2, 4, D_MODEL, 256).transpose(0, 1, 3, 2, 4).reshape(4, 2, D_MODEL, 1024)

    w13 = jnp.concatenate([half(w1), half(w3)], axis=-1).astype(bf16)
    w2h = w2.reshape(4, 2, 1024, D_MODEL).astype(bf16)
    ex = np.zeros((4, 2, 128, 4, 256), np.float32)
    for g in range(4):
        for s in range(2):
            for e in range(4):
                ex[g, s, g * 8 + s * 4 + e, e, :] = 1.0
    return wr, br, w13, w2h, jnp.asarray(ex.reshape(4, 2, 128, 1024)).astype(bf16)


def _dft_tables():
    two_pi = 2.0 * math.pi
    k1 = jnp.arange(K1, dtype=jnp.int32)
    n1 = jnp.arange(N1P, dtype=jnp.int32)
    th = two_pi * ((k1[:, None] * n1[None, :]) % N1).astype(f32) / N1
    f1c = jnp.concatenate([jnp.cos(th), -jnp.sin(th)], axis=0).astype(bf16)
    wgt = jnp.where((k1 == 0) | (k1 == N1 // 2), 1.0, 2.0).astype(f32) / NFFT
    f1inv = jnp.concatenate([jnp.cos(th).T * wgt[None, :], -jnp.sin(th).T * wgt[None, :]], axis=1).astype(bf16)
    n2 = jnp.arange(N2, dtype=jnp.int32)
    a2 = -two_pi * ((n2[:, None] * n2[None, :]) % N2).astype(f32) / N2
    f2 = jnp.stack([jnp.cos(a2), jnp.sin(a2)])
    at = -two_pi * ((k1[:, None] * n2[None, :]) % NFFT).astype(f32) / NFFT
    tw = jnp.stack([jnp.cos(at), jnp.sin(at)], axis=1)
    return f1c, f1inv, f2, tw


def _hyena(hy_in, short_w, short_b, f_w1, f_b1, f_w2, f_b2, f_w3, sin_freq, log_decay, hy_bias):
    v, x1, x2 = _short(hy_in, short_w, short_b[None])
    f1c, f1inv, f2, tw = _dft_tables()
    tt = jnp.linspace(0.0, 1.0, T, dtype=f32)
    bands = jnp.linspace(1e-4, 15.0, 16, dtype=f32)
    ang = 2.0 * math.pi * tt[:, None] * bands[None, :]
    feats = jnp.concatenate([tt[:, None], jnp.cos(ang), jnp.sin(ang), jnp.zeros((T, 95), f32)], axis=-1)
    feats = jnp.concatenate([feats, jnp.zeros((THY - T, 128), f32)], axis=0)
    w1p = jnp.concatenate([f_w1, jnp.zeros((95, 64), f32)], axis=0)
    rate = jnp.exp(log_decay.astype(f32)).reshape(1, 2048)
    filt = _filt(feats, w1p, f_b1[None], f_w2, f_b2[None], f_w3, sin_freq[None], rate)
    kf = _spec_filt(_dft1(f1c, filt), f2, tw)
    z = v
    for o, gate in enumerate((x1, x2)):
        zf = _spec_conv(_dft1(f1c, z), kf, o, f2, tw)
        z = _idft(f1inv, zf, z, gate, hy_bias[o][None])
    return z


def kernel(x, meta_tokens, norm_mix, norm_ffn, norm_final, a_w_in, mla_q_norm, mla_w_uq, mla_kv_norm, mla_w_ukv, nat_rpb, a_w_out, c_w_in, hy_short_w, hy_short_b, hy_ffn_w1, hy_ffn_b1, hy_ffn_w2, hy_ffn_b2, hy_ffn_w3, hy_sin_freq, hy_log_decay, hy_bias, cf_dw_w, cf_dw_b, cf_ln_g, cf_ln_b, c_w_out, moe_w_group, moe_b_group, moe_w_router, moe_b_router, moe_w1, moe_w3, moe_w2):
    h = jnp.concatenate([meta_tokens, x[0], jnp.zeros((TP - T, D_MODEL), f32)], axis=0)

    w0 = _prep_in0(a_w_in[0], mla_q_norm[0], mla_w_uq[0], mla_kv_norm[0], mla_w_ukv[0])
    qt, k, vt, nq, nk, nv = _in0(h, norm_mix[0][None], w0)
    mla_t = _mla(qt, k, vt).reshape(HEADS * 64, TP)
    nat = _nat(nq, nk, nv, _nat_bias(nat_rpb[0]))
    nat = nat.transpose(1, 0, 2).reshape(TP, HEADS * 64)
    wo = a_w_out[0].astype(bf16)
    wr, br, w13, w2h, ex = _prep_moe(moe_w_group[0], moe_b_group[0], moe_w_router[0], moe_b_router[0],
                                     moe_w1[0], moe_w3[0], moe_w2[0])
    h, xn, gate = _out_router(h, mla_t, nat, wo[:512], wo[512:], norm_ffn[0][None], wr, br, a_transposed=True)
    h = _moe(xn, gate, h, w13, w2h, ex)

    hy_in, cf_in = _in1(h, norm_mix[1][None], c_w_in[0].astype(bf16))
    z = _hyena(hy_in, hy_short_w[0], hy_short_b[0], hy_ffn_w1[0], hy_ffn_b1[0], hy_ffn_w2[0], hy_ffn_b2[0],
               hy_ffn_w3[0], hy_sin_freq[0], hy_log_decay[0], hy_bias[0])
    c = _conf(cf_in, cf_dw_w[0], cf_dw_b[0][None], cf_ln_g[0][None], cf_ln_b[0][None])
    wo = c_w_out[0].astype(bf16)
    wr, br, w13, w2h, ex = _prep_moe(moe_w_group[1], moe_b_group[1], moe_w_router[1], moe_b_router[1],
                                     moe_w1[1], moe_w3[1], moe_w2[1])
    h, xn, gate = _out_router(h, z, c, wo[:512], wo[512:], norm_ffn[1][None], wr, br)
    h = _moe(xn, gate, h, w13, w2h, ex)

    return _final(h, norm_final[None])[None]
```

```python
import functools
import math

import numpy as np
import jax
import jax.numpy as jnp
from jax import lax
from jax.experimental import pallas as pl
from jax.experimental.pallas import tpu as pltpu

f32 = jnp.float32
bf16 = jnp.bfloat16

D_MODEL = 1024
SEQ = 16384
N_META = 16
T = SEQ + N_META
TM = 512
TP = 33 * TM
NB = TP // TM
EPS = 1e-6
HEADS = 8
MLA_DK = 96
GRID_W = 64
ROWS = SEQ // GRID_W
WIN_ROWS = 8
WIN_COLS = 16
NEG = -1e30
VT_ROWS = 80
NAT_UNROLL = 8
XG_W = D_MODEL + 128
SORT_TILE = 512
SP = (NB + 4) * SORT_TILE
MLA_UNROLL = 8

HY_W = 512
CF_K = 31
N2 = 128
N1 = 286
NFFT = N1 * N2
K1 = N1 // 2 + 1
N1P = 144
THY = N1P * N2

VMEM_LIMIT = 56 * 1024 * 1024


def _cp(sem, vmem=VMEM_LIMIT):
    return pltpu.CompilerParams(dimension_semantics=sem, vmem_limit_bytes=vmem)


def _rms(x, g):
    return x * lax.rsqrt(jnp.mean(x * x, axis=-1, keepdims=True) + EPS) * g


def _dot(a, b):
    return jnp.dot(a, b, preferred_element_type=f32)


def _dot_hi(a, b):
    return jnp.dot(a, b, preferred_element_type=f32, precision=lax.Precision.HIGHEST)


def _k_in0(h_ref, g_ref, win_ref, qn_ref, kvn_ref, wq_ref, wqr_ref, wk_ref, e_ref, er_ref, wv_ref, cos_ref, sin_ref,
           qt_out, k_out, vt_out, nq_out, nk_out, nv_out):
    i = pl.program_id(0)
    hn = _rms(h_ref[...], g_ref[...]).astype(bf16)
    p = _dot(hn, win_ref[...])
    for hd in range(HEADS):
        nq_out[hd] = p[:, 512 + 64 * hd:576 + 64 * hd].astype(bf16)
        nk_out[hd] = p[:, 1024 + 64 * hd:1088 + 64 * hd].astype(bf16)
        nv_out[hd] = p[:, 1536 + 64 * hd:1600 + 64 * hd].astype(bf16)
    cqn = _rms(p[:, 0:256], qn_ref[...]).astype(bf16)
    ckvn = _rms(p[:, 256:384], kvn_ref[...]).astype(bf16)
    kpe = p[:, 384:512].astype(bf16)
    qa = _dot(cqn, wq_ref[...])
    qr = _dot(cqn, wqr_ref[...])
    ka = _dot(ckvn, wk_ref[...]) + _dot(kpe, e_ref[...])
    kr = _dot(kpe, er_ref[...])
    va = _dot(ckvn, wv_ref[...])
    cos = cos_ref[...]
    sin = sin_ref[...]
    t = i * TM + lax.broadcasted_iota(jnp.int32, (TM, 1), 0)
    lane = lax.broadcasted_iota(jnp.int32, (1, 128), 1)
    ones_col = jnp.where((t < T) & (lane == 64), 1.0, 0.0)
    for hd in range(HEADS):
        sl = slice(hd * 128, (hd + 1) * 128)
        qt_out[sl, :] = (qa[:, sl] * cos + qr[:, sl] * sin).T.astype(bf16)
        k_out[hd] = (ka[:, sl] * cos + kr[:, sl] * sin).astype(bf16)
        vt_out[hd, 0] = (va[:, sl] + ones_col).T[:VT_ROWS].astype(bf16)


def _in0(h, g, w):
    full = lambda shape: pl.BlockSpec(shape, lambda i: (0,) * len(shape))
    row = lambda n: pl.BlockSpec((TM, n), lambda i: (i, 0))
    hm = lambda n: pl.BlockSpec((HEADS, TM, n), lambda i: (0, i, 0))
    outs = [jax.ShapeDtypeStruct((HEADS * 128, TP), bf16), jax.ShapeDtypeStruct((HEADS, TP, 128), bf16),
            jax.ShapeDtypeStruct((HEADS, NB, VT_ROWS, TM), bf16)] + [jax.ShapeDtypeStruct((HEADS, TP, 64), bf16)] * 3
    return pl.pallas_call(
        _k_in0,
        grid=(NB,),
        in_specs=[row(1024), full((1, 1024)), full((1024, 2048)), full((1, 256)), full((1, 128)),
                  full((256, 1024)), full((256, 1024)), full((128, 1024)), full((128, 1024)), full((128, 1024)),
                  full((128, 1024)), row(128), row(128)],
        out_specs=[pl.BlockSpec((HEADS * 128, TM), lambda i: (0, i)), hm(128),
                   pl.BlockSpec((HEADS, 1, VT_ROWS, TM), lambda i: (0, i, 0, 0)), hm(64), hm(64), hm(64)],
        out_shape=outs,
        compiler_params=_cp(("parallel",)),
        name="in0",
    )(h, g, w["win"], w["qn"], w["kvn"], w["wq"], w["wqr"], w["wk"], w["e"], w["er"], w["wv"], w["cos"], w["sin"])


def _k_mla(qt_ref, k_ref, vt_ref, o_ref, sa_ref, sb_ref, m_ref, acc_ref):
    qt = qt_ref[...]
    m_ref[...] = jnp.full((1, TM), NEG, f32)
    acc_ref[...] = jnp.zeros((VT_ROWS, TM), f32)

    def scores(j):
        return _dot(k_ref[0, j], qt)

    def update(s_ref, j, masked):
        s = s_ref[...]
        if masked:
            key = lax.broadcasted_iota(jnp.int32, (TM, 1), 0)
            s = jnp.where(key < T - (NB - 1) * TM, s, NEG)
        m = m_ref[...]
        m_new = jnp.maximum(m, s.max(axis=0, keepdims=True))
        p = jnp.exp2(s - m_new).astype(bf16)
        acc_ref[...] = jnp.exp2(m - m_new) * acc_ref[...] + _dot(vt_ref[0, j], p)
        m_ref[...] = m_new

    sa_ref[...] = scores(0)

    def group(i, _):
        for u in range(0, MLA_UNROLL, 2):
            j = MLA_UNROLL * i + u
            sb_ref[...] = scores(j + 1)
            update(sa_ref, j, False)
            sa_ref[...] = scores(j + 2)
            update(sb_ref, j + 1, False)
        return 0

    lax.fori_loop(0, (NB - 1) // MLA_UNROLL, group, 0)
    update(sa_ref, NB - 1, True)
    acc = acc_ref[...]
    o_ref[0] = (acc[:64] / acc[64:65]).astype(bf16)


def _mla(qt, k, vt):
    return pl.pallas_call(
        _k_mla,
        grid=(HEADS, NB),
        in_specs=[pl.BlockSpec((128, TM), lambda h, i: (h, i)),
                  pl.BlockSpec((1, NB, TM, 128), lambda h, i: (h, 0, 0, 0)),
                  pl.BlockSpec((1, NB, VT_ROWS, TM), lambda h, i: (h, 0, 0, 0))],
        out_specs=pl.BlockSpec((1, 64, TM), lambda h, i: (h, 0, i)),
        out_shape=jax.ShapeDtypeStruct((HEADS, 64, TP), bf16),
        scratch_shapes=[pltpu.VMEM((TM, TM), f32), pltpu.VMEM((TM, TM), f32), pltpu.VMEM((1, TM), f32),
                        pltpu.VMEM((VT_ROWS, TM), f32)],
        compiler_params=_cp(("parallel", "parallel")),
        name="mla",
    )(qt, k.reshape(HEADS, NB, TM, 128), vt)


def _k_nat(q_ref, k_ref, v_ref, c_ref, o_ref):
    km = k_ref[0, 0:N_META, :]
    vm = v_ref[0, 0:N_META, :]
    qm = q_ref[0, 0:N_META, :]
    s = lax.dot_general(qm, km, (((1,), (1,)), ((), ())), preferred_element_type=f32)
    p = jnp.exp(s - s.max(axis=1, keepdims=True))
    o_ref[0, 0:N_META, :] = (_dot(p.astype(bf16), vm) / p.sum(axis=1, keepdims=True)).astype(bf16)
    o_ref[0, T:TP, :] = jnp.zeros((TP - T, 64), bf16)

    def rows(ii, _):
        idx = [ii * NAT_UNROLL + r for r in range(NAT_UNROLL)]
        r0s = [jnp.clip(i - WIN_ROWS // 2, 0, ROWS - WIN_ROWS) for i in idx]
        qoffs = [pl.multiple_of(N_META + GRID_W * i, 16) for i in idx]
        koffs = [pl.multiple_of(N_META + GRID_W * r0, 16) for r0 in r0s]
        ss, sms = [], []
        for i, r0, qoff, koff in zip(idx, r0s, qoffs, koffs):
            a0 = r0 - i + WIN_ROWS - 1
            qi = q_ref[0, pl.ds(qoff, GRID_W), :]
            kw = k_ref[0, pl.ds(koff, WIN_ROWS * GRID_W), :]
            s = lax.dot_general(qi, kw, (((1,), (1,)), ((), ())), preferred_element_type=f32)
            ss.append(s + jnp.concatenate([c_ref[0, a0 + 2 * j] for j in range(WIN_ROWS // 2)], axis=1))
            sms.append(lax.dot_general(qi, km, (((1,), (1,)), ((), ())), preferred_element_type=f32))
        ps, pms, ls = [], [], []
        for s, sm in zip(ss, sms):
            m = jnp.maximum(s.max(axis=1, keepdims=True), sm.max(axis=1, keepdims=True))
            p = jnp.exp(s - m)
            pm = jnp.exp(sm - m)
            ls.append(p.sum(axis=1, keepdims=True) + pm.sum(axis=1, keepdims=True))
            ps.append(p.astype(bf16))
            pms.append(pm.astype(bf16))
        for p, pm, l, qoff, koff in zip(ps, pms, ls, qoffs, koffs):
            vw = v_ref[0, pl.ds(koff, WIN_ROWS * GRID_W), :]
            o = _dot(p, vw) + _dot(pm, vm)
            o_ref[0, pl.ds(qoff, GRID_W), :] = (o / l).astype(bf16)
        return 0

    lax.fori_loop(0, ROWS // NAT_UNROLL, rows, 0)


def _nat(q, k, v, c2):
    blk = pl.BlockSpec((1, TP, 64), lambda h: (h, 0, 0))
    return pl.pallas_call(
        _k_nat,
        grid=(HEADS,),
        in_specs=[blk, blk, blk, pl.BlockSpec((1, 2 * WIN_ROWS - 2, GRID_W, 2 * GRID_W), lambda h: (h, 0, 0, 0))],
        out_specs=blk,
        out_shape=jax.ShapeDtypeStruct((HEADS, TP, 64), bf16),
        compiler_params=_cp(("parallel",)),
        name="nat",
    )(q, k, v, c2)


def _k_out_router(h_ref, a_ref, b_ref, wa_ref, wb_ref, g_ref, wr_ref, br_ref, h_out, xg_out, gidx_out, *, a_transposed):
    if a_transposed:
        ya = lax.dot_general(a_ref[...], wa_ref[...], (((0,), (0,)), ((), ())), preferred_element_type=f32)
    else:
        ya = _dot(a_ref[...].reshape(TM, 512).astype(bf16), wa_ref[...])
    h1 = h_ref[...] + ya + _dot(b_ref[...].astype(bf16), wb_ref[...])
    h_out[...] = h1
    xn = _rms(h1, g_ref[...])
    xg_out[:, :D_MODEL] = xn
    logits = _dot_hi(xn, wr_ref[...]) + br_ref[...]
    lane = lax.broadcasted_iota(jnp.int32, logits.shape, 1)
    big = jnp.int32(1 << 20)
    gl = jnp.where((lane >= 32) & (lane < 36), logits, NEG)
    gmax = gl.max(axis=1, keepdims=True)
    gsum = jnp.exp(gl - gmax).sum(axis=1, keepdims=True)
    p_group = 1.0 / gsum
    gidx = jnp.clip(jnp.where(gl == gmax, lane, big).min(axis=1, keepdims=True) - 32, 0, 3)
    el = jnp.where((lane >> 3) == gidx, logits, NEG)
    v1 = el.max(axis=1, keepdims=True)
    i1 = jnp.where(el == v1, lane, big).min(axis=1, keepdims=True)
    el2 = jnp.where(lane == i1, NEG, el)
    v2 = el2.max(axis=1, keepdims=True)
    i2 = jnp.where(el2 == v2, lane, big).min(axis=1, keepdims=True)
    e2 = jnp.exp(v2 - v1)
    w1 = p_group / (1.0 + e2)
    w2 = w1 * e2
    xg_out[:, D_MODEL:] = jnp.where(lane == i1, w1, jnp.where(lane == i2, w2, 0.0))
    gidx_out[...] = gidx


def _out_router(h, a, b, wa, wb, g, wr, br, a_transposed=False):
    full = lambda shape: pl.BlockSpec(shape, lambda i: (0,) * len(shape))
    row = lambda n: pl.BlockSpec((TM, n), lambda i: (i, 0))
    a_spec = (pl.BlockSpec((512, TM), lambda i: (0, i)) if a_transposed
              else pl.BlockSpec((TM // N2, N2, 512), lambda i: (i, 0, 0)))
    return pl.pallas_call(
        functools.partial(_k_out_router, a_transposed=a_transposed),
        grid=(NB,),
        in_specs=[row(1024), a_spec, row(512), full((512, 1024)), full((512, 1024)), full((1, 1024)),
                  full((1024, 128)), full((1, 128))],
        out_specs=[row(1024), row(XG_W), row(1)],
        out_shape=[jax.ShapeDtypeStruct((TP, 1024), f32), jax.ShapeDtypeStruct((TP, XG_W), f32),
                   jax.ShapeDtypeStruct((TP, 1), jnp.int32)],
        compiler_params=_cp(("parallel",)),
        name="out_router",
    )(h, a, b, wa, wb, g, wr, br)


def _k_rank(g_ref, rank_out, cnt_out, carry_ref):
    @pl.when(pl.program_id(0) == 0)
    def _():
        carry_ref[...] = jnp.zeros((1, 128), f32)

    lane = lax.broadcasted_iota(jnp.int32, (TM, 128), 1)
    oh = jnp.where(lane == g_ref[...], 1.0, 0.0)
    r = lax.broadcasted_iota(jnp.int32, (TM, TM), 0)
    c = lax.broadcasted_iota(jnp.int32, (TM, TM), 1)
    earlier = jnp.where(c < r, 1.0, 0.0).astype(bf16)
    excl = _dot(earlier, oh.astype(bf16)) + carry_ref[...]
    rank_out[...] = (oh * excl).sum(axis=1, keepdims=True).astype(jnp.int32)
    carry_ref[...] += oh.sum(axis=0, keepdims=True)
    cnt_out[...] = carry_ref[...]


def _rank(gidx):
    return pl.pallas_call(
        _k_rank,
        grid=(NB,),
        in_specs=[pl.BlockSpec((TM, 1), lambda i: (i, 0))],
        out_specs=[pl.BlockSpec((TM, 1), lambda i: (i, 0)), pl.BlockSpec((1, 128), lambda i: (0, 0))],
        out_shape=[jax.ShapeDtypeStruct((TP, 1), jnp.int32), jax.ShapeDtypeStruct((1, 128), f32)],
        scratch_shapes=[pltpu.VMEM((1, 128), f32)],
        compiler_params=_cp(("arbitrary",)),
        name="moe_rank",
    )(gidx)


def _row_copy(src_ref, src_row, dst_ref, dst_row, sem):
    return pltpu.make_async_copy(src_ref.at[pl.ds(src_row, 1)], dst_ref.at[pl.ds(dst_row, 1)], sem)


def _k_dispatch(pos_ref, x_ref, xs_in_ref, xs_ref, sem):
    del xs_in_ref
    base = pl.program_id(0) * TM

    def start(r, _):
        _row_copy(x_ref, r, xs_ref, pos_ref[base + r], sem).start()
        return 0

    def wait(r, _):
        _row_copy(x_ref, 0, xs_ref, 0, sem).wait()
        return 0

    lax.fori_loop(0, TM, start, 0, unroll=8)
    lax.fori_loop(0, TM, wait, 0, unroll=8)


def _dispatch(pos, xg):
    return pl.pallas_call(
        _k_dispatch,
        grid_spec=pltpu.PrefetchScalarGridSpec(
            num_scalar_prefetch=1, grid=(NB,),
            in_specs=[pl.BlockSpec((TM, XG_W), lambda i, pos: (i, 0)), pl.BlockSpec(memory_space=pl.ANY)],
            out_specs=pl.BlockSpec(memory_space=pl.ANY),
            scratch_shapes=[pltpu.SemaphoreType.DMA(())]),
        out_shape=jax.ShapeDtypeStruct((SP, XG_W), f32),
        input_output_aliases={2: 0},
        compiler_params=_cp(("arbitrary",)),
        name="moe_dispatch",
    )(pos, xg, jnp.zeros((SP, XG_W), f32))


def _k_moe(tg_ref, nt_ref, xs_ref, w13_ref, w2_ref, ex_ref, ys_ref):
    del tg_ref
    live = pl.program_id(0) < nt_ref[0]
    half = pl.program_id(1)

    @pl.when(jnp.logical_not(live) & (half == 0))
    def _():
        ys_ref[...] = jnp.zeros((SORT_TILE, D_MODEL), f32)

    @pl.when(live)
    def _():
        x = xs_ref[...]
        a = _dot(x[:, :D_MODEL].astype(bf16), w13_ref[0, 0])
        gate = x[:, D_MODEL:]
        ghi = gate.astype(bf16)
        glo = (gate - ghi.astype(f32)).astype(bf16)
        gx = _dot(ghi, ex_ref[0, 0]) + _dot(glo, ex_ref[0, 0])
        a1 = a[:, :1024]
        hid = a1 * jax.nn.sigmoid(a1) * a[:, 1024:] * gx
        y = _dot(hid.astype(bf16), w2_ref[0, 0])

        @pl.when(half == 0)
        def _():
            ys_ref[...] = y

        @pl.when(half == 1)
        def _():
            ys_ref[...] += y


def _moe_sorted(tile_group, n_tiles, xs, w13, w2, ex):
    wspec = lambda shape: pl.BlockSpec((1, 1) + shape, lambda j, s, tg, nt: (tg[j], s, 0, 0))
    return pl.pallas_call(
        _k_moe,
        grid_spec=pltpu.PrefetchScalarGridSpec(
            num_scalar_prefetch=2, grid=(SP // SORT_TILE, 2),
            in_specs=[pl.BlockSpec((SORT_TILE, XG_W), lambda j, s, tg, nt: (j, 0)),
                      wspec((1024, 2048)), wspec((1024, 1024)), wspec((128, 1024))],
            out_specs=pl.BlockSpec((SORT_TILE, D_MODEL), lambda j, s, tg, nt: (j, 0))),
        out_shape=jax.ShapeDtypeStruct((SP, D_MODEL), f32),
        compiler_params=_cp(("arbitrary", "arbitrary")),
        name="moe",
    )(tile_group, n_tiles, xs, w13, w2, ex)


def _k_combine(pos_ref, h_ref, ys_ref, o_ref, buf_ref, sem):
    base = pl.program_id(0) * TM

    def start(r, _):
        _row_copy(ys_ref, pos_ref[base + r], buf_ref, r, sem).start()
        return 0

    def wait(r, _):
        _row_copy(ys_ref, 0, buf_ref, 0, sem).wait()
        return 0

    lax.fori_loop(0, TM, start, 0, unroll=8)
    lax.fori_loop(0, TM, wait, 0, unroll=8)
    o_ref[...] = h_ref[...] + buf_ref[...]


def _combine(pos, h, ys):
    return pl.pallas_call(
        _k_combine,
        grid_spec=pltpu.PrefetchScalarGridSpec(
            num_scalar_prefetch=1, grid=(NB,),
            in_specs=[pl.BlockSpec((TM, D_MODEL), lambda i, pos: (i, 0)), pl.BlockSpec(memory_space=pl.ANY)],
            out_specs=pl.BlockSpec((TM, D_MODEL), lambda i, pos: (i, 0)),
            scratch_shapes=[pltpu.VMEM((TM, D_MODEL), f32), pltpu.SemaphoreType.DMA(())]),
        out_shape=jax.ShapeDtypeStruct((TP, D_MODEL), f32),
        compiler_params=_cp(("arbitrary",)),
        name="moe_combine",
    )(pos, h, ys)


def _moe(xg, gidx, h, w13, w2, ex):
    rank, cnt = _rank(gidx)
    cnt = cnt[0, :4].astype(jnp.int32)
    tiles = (cnt + SORT_TILE - 1) // SORT_TILE
    tile_end = jnp.cumsum(tiles)
    start = (tile_end - tiles) * SORT_TILE
    pos = start[gidx[:, 0]] + rank[:, 0]
    tile_group = jnp.minimum(jnp.sum(jnp.arange(SP // SORT_TILE)[:, None] >= tile_end[None, :], axis=1), 3)
    xs = _dispatch(pos, xg)
    ys = _moe_sorted(tile_group.astype(jnp.int32), tile_end[3:4].astype(jnp.int32), xs, w13, w2, ex)
    return _combine(pos, h, ys)


def _k_in1(h_ref, g_ref, w_ref, hy_out, cf_out):
    hn = _rms(h_ref[...], g_ref[...]).astype(bf16)
    p = _dot(hn, w_ref[...])
    hy_out[...] = p[:, :1536]
    cf_out[...] = p[:, 1536:]


def _in1(h, g, w):
    row = lambda n: pl.BlockSpec((TM, n), lambda i: (i, 0))
    return pl.pallas_call(
        _k_in1,
        grid=(NB,),
        in_specs=[row(1024), pl.BlockSpec((1, 1024), lambda i: (0, 0)), pl.BlockSpec((1024, 2560), lambda i: (0, 0))],
        out_specs=[row(1536), row(1024)],
        out_shape=[jax.ShapeDtypeStruct((TP, 1536), f32), jax.ShapeDtypeStruct((TP, 1024), f32)],
        compiler_params=_cp(("parallel",)),
        name="in1",
    )(h, g, w)


def _k_short(prev_ref, cur_ref, next_ref, w_ref, b_ref, v_out, x1_out, x2_out):
    i = pl.program_id(0)
    ext = jnp.concatenate([prev_ref[...], cur_ref[...], next_ref[...]], axis=0)
    t = i * TM - 8 + lax.broadcasted_iota(jnp.int32, (TM + 16, 1), 0)
    ext = jnp.where((t >= 0) & (t < T), ext, 0.0)
    w = w_ref[...]
    s = ext[7:7 + TM] * w[0:1] + ext[8:8 + TM] * w[1:2] + ext[9:9 + TM] * w[2:3] + b_ref[...]
    s = jnp.where(t[8:8 + TM] < T, s, 0.0)
    v_out[...] = s[:, :512].reshape(TM // N2, N2, 512)
    x1_out[...] = s[:, 512:1024].reshape(TM // N2, N2, 512)
    x2_out[...] = s[:, 1024:].reshape(TM // N2, N2, 512)


def _short(hy, w, b):
    last = NB - 1
    nblk = THY // TM
    return pl.pallas_call(
        _k_short,
        grid=(nblk,),
        in_specs=[pl.BlockSpec((8, 1536), lambda i: (jnp.maximum(jnp.minimum(i, last) * (TM // 8) - 1, 0), 0)),
                  pl.BlockSpec((TM, 1536), lambda i: (jnp.minimum(i, last), 0)),
                  pl.BlockSpec((8, 1536), lambda i: (jnp.minimum((jnp.minimum(i, last) + 1) * (TM // 8), TP // 8 - 1), 0)),
                  pl.BlockSpec((3, 1536), lambda i: (0, 0)), pl.BlockSpec((1, 1536), lambda i: (0, 0))],
        out_specs=[pl.BlockSpec((TM // N2, N2, 512), lambda i: (i, 0, 0))] * 3,
        out_shape=[jax.ShapeDtypeStruct((N1P, N2, 512), f32)] * 3,
        compiler_params=_cp(("parallel",)),
        name="hy_short",
    )(hy, hy, hy, w, b)


def _k_filt(feat_ref, w1_ref, b1_ref, w2_ref, b2_ref, w3_ref, fr_ref, rate_ref, o_ref):
    i = pl.program_id(0)
    feats = feat_ref[...]
    fr = fr_ref[...]
    hid = jnp.sin(fr * (_dot_hi(feats, w1_ref[...]) + b1_ref[...]))
    hid = jnp.sin(fr * (_dot_hi(hid, w2_ref[...]) + b2_ref[...]))
    filt = _dot(hid.astype(bf16), w3_ref[...].astype(bf16))
    filt = filt * jnp.exp(-feats[:, 0:1] * rate_ref[...])
    lag = i * TM + lax.broadcasted_iota(jnp.int32, (TM, 1), 0)
    col = lax.broadcasted_iota(jnp.int32, (1, 2048), 1)
    keep = (lag < T) & ((lag > 0) | (col < 1024))
    o_ref[...] = jnp.where(keep, filt, 0.0).reshape(TM // N2, N2, 2048)


def _filt(feats, w1, b1, w2, b2, w3, fr, rate):
    full = lambda shape: pl.BlockSpec(shape, lambda i: (0,) * len(shape))
    return pl.pallas_call(
        _k_filt,
        grid=(THY // TM,),
        in_specs=[pl.BlockSpec((TM, 128), lambda i: (i, 0)), full((128, 64)), full((1, 64)), full((64, 64)),
                  full((1, 64)), full((64, 2048)), full((1, 64)), full((1, 2048))],
        out_specs=pl.BlockSpec((TM // N2, N2, 2048), lambda i: (i, 0, 0)),
        out_shape=jax.ShapeDtypeStruct((N1P, N2, 2048), f32),
        compiler_params=_cp(("parallel",)),
        name="hy_filt",
    )(feats, w1, b1, w2, b2, w3, fr, rate)


def _k_dft1(f_ref, x_ref, y_ref):
    f = f_ref[...]
    for j in range(8):
        y = _dot(f, x_ref[:, j, :].astype(bf16))
        y_ref[:, 0, j, :] = y[:K1]
        y_ref[:, 1, j, :] = y[K1:]


def _dft1(f1c, x3):
    c = x3.shape[2]
    return pl.pallas_call(
        _k_dft1,
        grid=(N2 // 8, c // 512),
        in_specs=[pl.BlockSpec((2 * K1, N1P), lambda i, cb: (0, 0)),
                  pl.BlockSpec((N1P, 8, 512), lambda i, cb: (0, i, cb))],
        out_specs=pl.BlockSpec((K1, 2, 8, 512), lambda i, cb: (0, 0, i, cb)),
        out_shape=jax.ShapeDtypeStruct((K1, 2, N2, c), f32),
        compiler_params=_cp(("parallel", "parallel")),
        name="dft1",
    )(f1c, x3)


def _stage2_matrix(f2_ref, tw_ref):
    fr, fi = f2_ref[0], f2_ref[1]
    tr, ti = tw_ref[0, 0:1, :], tw_ref[0, 1:2, :]
    gre = fr * tr - fi * ti
    gim = fr * ti + fi * tr
    return jnp.concatenate([jnp.concatenate([gre, -gim], axis=1), jnp.concatenate([gim, gre], axis=1)], axis=0)


def _k_spec_filt(y_ref, f2_ref, tw_ref, o_ref):
    y = y_ref[0].reshape(2 * N2, 2048).astype(bf16)
    x = _dot(_stage2_matrix(f2_ref, tw_ref).astype(bf16), y)
    o_ref[0, 0] = x[:N2, :1024] + x[:N2, 1024:]
    o_ref[0, 1] = x[N2:, :1024] - x[N2:, 1024:]


def _spec_filt(yf, f2, tw):
    return pl.pallas_call(
        _k_spec_filt,
        grid=(K1,),
        in_specs=[pl.BlockSpec((1, 2, N2, 2048), lambda k: (k, 0, 0, 0)),
                  pl.BlockSpec((2, N2, N2), lambda k: (0, 0, 0)), pl.BlockSpec((1, 2, N2), lambda k: (k, 0, 0))],
        out_specs=pl.BlockSpec((1, 2, N2, 1024), lambda k: (k, 0, 0, 0)),
        out_shape=jax.ShapeDtypeStruct((K1, 2, N2, 1024), f32),
        compiler_params=_cp(("parallel",)),
        name="spec_filt",
    )(yf, f2, tw)


def _k_spec_conv(y_ref, kf_ref, f2_ref, tw_ref, z_ref):
    g = _stage2_matrix(f2_ref, tw_ref)
    y = y_ref[0].reshape(2 * N2, 512).astype(bf16)
    x = _dot(g.astype(bf16), y)
    xr, xi = x[:N2], x[N2:]
    kr, ki = kf_ref[0, 0], kf_ref[0, 1]
    p = jnp.concatenate([xr * kr - xi * ki, xr * ki + xi * kr], axis=0).astype(bf16)
    z_ref[0] = _dot(g.T.astype(bf16), p).reshape(2, N2, 512)


def _spec_conv(y, kf, order, f2, tw):
    return pl.pallas_call(
        _k_spec_conv,
        grid=(K1,),
        in_specs=[pl.BlockSpec((1, 2, N2, 512), lambda k: (k, 0, 0, 0)),
                  pl.BlockSpec((1, 2, N2, 512), lambda k: (k, 0, 0, order)),
                  pl.BlockSpec((2, N2, N2), lambda k: (0, 0, 0)), pl.BlockSpec((1, 2, N2), lambda k: (k, 0, 0))],
        out_specs=pl.BlockSpec((1, 2, N2, 512), lambda k: (k, 0, 0, 0)),
        out_shape=jax.ShapeDtypeStruct((K1, 2, N2, 512), f32),
        compiler_params=_cp(("parallel",)),
        name="spec_conv",
    )(y, kf, f2, tw)


def _k_idft(f_ref, z_ref, zin_ref, gate_ref, bias_ref, o_ref):
    f = f_ref[...]
    bias = bias_ref[...]
    for j in range(8):
        zj = jnp.concatenate([z_ref[:, 0, j, :], z_ref[:, 1, j, :]], axis=0).astype(bf16)
        y = _dot(f, zj)
        o_ref[:, j, :] = gate_ref[:, j, :] * (y + zin_ref[:, j, :] * bias)


def _idft(f1inv, z, zin, gate, bias):
    blk = pl.BlockSpec((N1P, 8, 512), lambda i: (0, i, 0))
    return pl.pallas_call(
        _k_idft,
        grid=(N2 // 8,),
        in_specs=[pl.BlockSpec((N1P, 2 * K1), lambda i: (0, 0)),
                  pl.BlockSpec((K1, 2, 8, 512), lambda i: (0, 0, i, 0)), blk, blk,
                  pl.BlockSpec((1, 512), lambda i: (0, 0))],
        out_specs=blk,
        out_shape=jax.ShapeDtypeStruct((N1P, N2, 512), f32),
        compiler_params=_cp(("parallel",)),
        name="idft",
    )(f1inv, z, zin, gate, bias)


def _k_conf(prev_ref, cur_ref, next_ref, w_ref, b_ref, lg_ref, lb_ref, o_ref):
    i = pl.program_id(0)
    ext = jnp.concatenate([prev_ref[...], cur_ref[...], next_ref[...]], axis=0)
    t = i * TM - 16 + lax.broadcasted_iota(jnp.int32, (TM + 32, 1), 0)
    u = ext[:, :512] * jax.nn.sigmoid(ext[:, 512:])
    u = jnp.where((t >= 0) & (t < T), u, 0.0)
    w = w_ref[...]
    acc = jnp.zeros((TM, 512), f32) + b_ref[...]
    for j in range(CF_K):
        acc = acc + u[j + 1:j + 1 + TM] * w[j:j + 1]
    mu = jnp.mean(acc, axis=-1, keepdims=True)
    xc = acc - mu
    y = xc * lax.rsqrt(jnp.mean(xc * xc, axis=-1, keepdims=True) + EPS) * lg_ref[...] + lb_ref[...]
    o_ref[...] = (y * jax.nn.sigmoid(y)).astype(bf16)


def _conf(cf, w, b, lg, lb):
    full = lambda shape: pl.BlockSpec(shape, lambda i: (0,) * len(shape))
    return pl.pallas_call(
        _k_conf,
        grid=(NB,),
        in_specs=[pl.BlockSpec((16, 1024), lambda i: (jnp.maximum(i * (TM // 16) - 1, 0), 0)),
                  pl.BlockSpec((TM, 1024), lambda i: (i, 0)),
                  pl.BlockSpec((16, 1024), lambda i: (jnp.minimum((i + 1) * (TM // 16), TP // 16 - 1), 0)),
                  full((CF_K, 512)), full((1, 512)), full((1, 512)), full((1, 512))],
        out_specs=pl.BlockSpec((TM, 512), lambda i: (i, 0)),
        out_shape=jax.ShapeDtypeStruct((TP, 512), bf16),
        compiler_params=_cp(("parallel",)),
        name="conformer",
    )(cf, cf, cf, w, b, lg, lb)


def _k_final(h_ref, nxt_ref, g_ref, o_ref):
    h = jnp.concatenate([h_ref[N_META:, :], nxt_ref[...]], axis=0)
    o_ref[...] = _rms(h, g_ref[...])


def _final(h, g):
    return pl.pallas_call(
        _k_final,
        grid=(SEQ // TM,),
        in_specs=[pl.BlockSpec((TM, 1024), lambda i: (i, 0)),
                  pl.BlockSpec((N_META, 1024), lambda i: ((i + 1) * (TM // N_META), 0)),
                  pl.BlockSpec((1, 1024), lambda i: (0, 0))],
        out_specs=pl.BlockSpec((TM, 1024), lambda i: (i, 0)),
        out_shape=jax.ShapeDtypeStruct((SEQ, 1024), f32),
        compiler_params=_cp(("parallel",)),
        name="final_norm",
    )(h, h, g)


def _prep_in0(a_w_in, q_norm, w_uq, kv_norm, w_ukv):
    win = jnp.concatenate([a_w_in[:, :416], jnp.zeros((D_MODEL, 96), f32), a_w_in[:, 416:928] * (64 ** -0.5),
                           a_w_in[:, 928:]], axis=1).astype(bf16)
    wq3 = (w_uq * (MLA_DK ** -0.5 * math.log2(math.e))).reshape(256, HEADS, MLA_DK)
    rope = wq3[..., 64:]
    rot = jnp.concatenate([-rope[..., 16:], rope[..., :16]], axis=-1)
    z64 = jnp.zeros((256, HEADS, 64), f32)
    z32 = jnp.zeros((256, HEADS, 32), f32)
    wq = jnp.concatenate([wq3, z32], axis=-1).reshape(256, 1024).astype(bf16)
    wqr = jnp.concatenate([z64, rot, z32], axis=-1).reshape(256, 1024).astype(bf16)
    kv3 = w_ukv.reshape(128, HEADS, 128)
    wk = jnp.concatenate([kv3[..., :64], jnp.zeros((128, HEADS, 64), f32)], axis=-1).reshape(128, 1024).astype(bf16)
    wv = jnp.concatenate([kv3[..., 64:], jnp.zeros((128, HEADS, 64), f32)], axis=-1).reshape(128, 1024).astype(bf16)
    e = np.zeros((128, HEADS, 128), np.float32)
    er = np.zeros((128, HEADS, 128), np.float32)
    for r in range(32):
        e[r, :, 64 + r] = 1.0
        if r < 16:
            er[r + 16, :, 64 + r] = -1.0
        else:
            er[r - 16, :, 64 + r] = 1.0
    pos = jnp.arange(TP, dtype=f32)
    inv_freq = 10000.0 ** (-jnp.arange(0, 32, 2, dtype=f32) / 32)
    ang = pos[:, None] * inv_freq[None, :]
    c, s = jnp.cos(ang), jnp.sin(ang)
    cos = jnp.concatenate([jnp.ones((TP, 64), f32), c, c, jnp.zeros((TP, 32), f32)], axis=1)
    sin = jnp.concatenate([jnp.zeros((TP, 64), f32), s, s, jnp.zeros((TP, 32), f32)], axis=1)
    return dict(win=win, qn=q_norm[None], kvn=kv_norm[None], wq=wq, wqr=wqr, wk=wk,
                e=jnp.asarray(e.reshape(128, 1024)).astype(bf16), er=jnp.asarray(er.reshape(128, 1024)).astype(bf16),
                wv=wv, cos=cos, sin=sin)


def _nat_bias(rpb):
    qc = np.arange(GRID_W)[:, None]
    kc = np.arange(GRID_W)[None, :]
    c0 = np.clip(qc - WIN_COLS // 2, 0, GRID_W - WIN_COLS)
    inside = (kc >= c0) & (kc < c0 + WIN_COLS)
    idx = np.clip(kc - qc + WIN_COLS - 1, 0, 2 * WIN_COLS - 2)
    c = jnp.where(jnp.asarray(inside)[None, None], rpb[:, :, idx], NEG)
    return jnp.concatenate([c[:, :-1], c[:, 1:]], axis=-1)


def _prep_moe(w_group, b_group, w_router, b_router, w1, w3, w2):
    wr = jnp.concatenate([w_router.transpose(1, 0, 2).reshape(D_MODEL, 32), w_group,
                          jnp.zeros((D_MODEL, 92), f32)], axis=1)
    br = jnp.concatenate([b_router.reshape(32), b_group, jnp.zeros((92,), f32)])[None]

    def half(w):
        return w.reshape(4, 2, 4, D_MODEL, 256).transpose(0, 1, 3, 2, 4).reshape(4, 2, D_MODEL, 1024)

    w13 = jnp.concatenate([half(w1), half(w3)], axis=-1).astype(bf16)
    w2h = w2.reshape(4, 2, 1024, D_MODEL).astype(bf16)
    ex = np.zeros((4, 2, 128, 4, 256), np.float32)
    for g in range(4):
        for s in range(2):
            for e in range(4):
                ex[g, s, g * 8 + s * 4 + e, e, :] = 1.0
    return wr, br, w13, w2h, jnp.asarray(ex.reshape(4, 2, 128, 1024)).astype(bf16)


def _dft_tables():
    two_pi = 2.0 * math.pi
    k1 = jnp.arange(K1, dtype=jnp.int32)
    n1 = jnp.arange(N1P, dtype=jnp.int32)
    th = two_pi * ((k1[:, None] * n1[None, :]) % N1).astype(f32) / N1
    f1c = jnp.concatenate([jnp.cos(th), -jnp.sin(th)], axis=0).astype(bf16)
    wgt = jnp.where((k1 == 0) | (k1 == N1 // 2), 1.0, 2.0).astype(f32) / NFFT
    f1inv = jnp.concatenate([jnp.cos(th).T * wgt[None, :], -jnp.sin(th).T * wgt[None, :]], axis=1).astype(bf16)
    n2 = jnp.arange(N2, dtype=jnp.int32)
    a2 = -two_pi * ((n2[:, None] * n2[None, :]) % N2).astype(f32) / N2
    f2 = jnp.stack([jnp.cos(a2), jnp.sin(a2)])
    at = -two_pi * ((k1[:, None] * n2[None, :]) % NFFT).astype(f32) / NFFT
    tw = jnp.stack([jnp.cos(at), jnp.sin(at)], axis=1)
    return f1c, f1inv, f2, tw


def _hyena(hy_in, short_w, short_b, f_w1, f_b1, f_w2, f_b2, f_w3, sin_freq, log_decay, hy_bias):
    v, x1, x2 = _short(hy_in, short_w, short_b[None])
    f1c, f1inv, f2, tw = _dft_tables()
    tt = jnp.linspace(0.0, 1.0, T, dtype=f32)
    bands = jnp.linspace(1e-4, 15.0, 16, dtype=f32)
    ang = 2.0 * math.pi * tt[:, None] * bands[None, :]
    feats = jnp.concatenate([tt[:, None], jnp.cos(ang), jnp.sin(ang), jnp.zeros((T, 95), f32)], axis=-1)
    feats = jnp.concatenate([feats, jnp.zeros((THY - T, 128), f32)], axis=0)
    w1p = jnp.concatenate([f_w1, jnp.zeros((95, 64), f32)], axis=0)
    rate = jnp.exp(log_decay.astype(f32)).reshape(1, 2048)
    filt = _filt(feats, w1p, f_b1[None], f_w2, f_b2[None], f_w3, sin_freq[None], rate)
    kf = _spec_filt(_dft1(f1c, filt), f2, tw)
    z = v
    for o, gate in enumerate((x1, x2)):
        zf = _spec_conv(_dft1(f1c, z), kf, o, f2, tw)
        z = _idft(f1inv, zf, z, gate, hy_bias[o][None])
    return z


def kernel(x, meta_tokens, norm_mix, norm_ffn, norm_final, a_w_in, mla_q_norm, mla_w_uq, mla_kv_norm, mla_w_ukv, nat_rpb, a_w_out, c_w_in, hy_short_w, hy_short_b, hy_ffn_w1, hy_ffn_b1, hy_ffn_w2, hy_ffn_b2, hy_ffn_w3, hy_sin_freq, hy_log_decay, hy_bias, cf_dw_w, cf_dw_b, cf_ln_g, cf_ln_b, c_w_out, moe_w_group, moe_b_group, moe_w_router, moe_b_router, moe_w1, moe_w3, moe_w2):
    h = jnp.concatenate([meta_tokens, x[0], jnp.zeros((TP - T, D_MODEL), f32)], axis=0)

    w0 = _prep_in0(a_w_in[0], mla_q_norm[0], mla_w_uq[0], mla_kv_norm[0], mla_w_ukv[0])
    qt, k, vt, nq, nk, nv = _in0(h, norm_mix[0][None], w0)
    mla_t = _mla(qt, k, vt).reshape(HEADS * 64, TP)
    nat = _nat(nq, nk, nv, _nat_bias(nat_rpb[0]))
    nat = nat.transpose(1, 0, 2).reshape(TP, HEADS * 64)
    wo = a_w_out[0].astype(bf16)
    wr, br, w13, w2h, ex = _prep_moe(moe_w_group[0], moe_b_group[0], moe_w_router[0], moe_b_router[0],
                                     moe_w1[0], moe_w3[0], moe_w2[0])
    h, xg, gidx = _out_router(h, mla_t, nat, wo[:512], wo[512:], norm_ffn[0][None], wr, br, a_transposed=True)
    h = _moe(xg, gidx, h, w13, w2h, ex)

    hy_in, cf_in = _in1(h, norm_mix[1][None], c_w_in[0].astype(bf16))
    z = _hyena(hy_in, hy_short_w[0], hy_short_b[0], hy_ffn_w1[0], hy_ffn_b1[0], hy_ffn_w2[0], hy_ffn_b2[0],
               hy_ffn_w3[0], hy_sin_freq[0], hy_log_decay[0], hy_bias[0])
    c = _conf(cf_in, cf_dw_w[0], cf_dw_b[0][None], cf_ln_g[0][None], cf_ln_b[0][None])
    wo = c_w_out[0].astype(bf16)
    wr, br, w13, w2h, ex = _prep_moe(moe_w_group[1], moe_b_group[1], moe_w_router[1], moe_b_router[1],
                                     moe_w1[1], moe_w3[1], moe_w2[1])
    h, xg, gidx = _out_router(h, z, c, wo[:512], wo[512:], norm_ffn[1][None], wr, br)
    h = _moe(xg, gidx, h, w13, w2h, ex)

    return _final(h, norm_final[None])[None]
```

```python
import functools
import math

import numpy as np
import jax
import jax.numpy as jnp
from jax import lax
from jax.experimental import pallas as pl
from jax.experimental.pallas import tpu as pltpu

f32 = jnp.float32
bf16 = jnp.bfloat16

D_MODEL = 1024
SEQ = 16384
N_META = 16
T = SEQ + N_META
TM = 512
TP = 33 * TM
NB = TP // TM
EPS = 1e-6
HEADS = 8
MLA_DK = 96
GRID_W = 64
ROWS = SEQ // GRID_W
WIN_ROWS = 8
WIN_COLS = 16
NEG = -1e30
VT_ROWS = 80
NAT_UNROLL = 16
XG_W = D_MODEL + 128
SORT_TILE = 512
SP = (NB + 4) * SORT_TILE
MLA_UNROLL = 8

HY_W = 512
CF_K = 31
N2 = 128
N1 = 286
NFFT = N1 * N2
K1 = N1 // 2 + 1
N1P = 144
THY = N1P * N2

VMEM_LIMIT = 56 * 1024 * 1024


def _cp(sem, vmem=VMEM_LIMIT):
    return pltpu.CompilerParams(dimension_semantics=sem, vmem_limit_bytes=vmem)


def _rms(x, g):
    return x * lax.rsqrt(jnp.mean(x * x, axis=-1, keepdims=True) + EPS) * g


def _dot(a, b):
    return jnp.dot(a, b, preferred_element_type=f32)


def _dot_hi(a, b):
    return jnp.dot(a, b, preferred_element_type=f32, precision=lax.Precision.HIGHEST)


def _k_in0(h_ref, g_ref, win_ref, qn_ref, kvn_ref, wq_ref, wqr_ref, wk_ref, e_ref, er_ref, wv_ref, cos_ref, sin_ref,
           qt_out, k_out, vt_out, nq_out, nk_out, nv_out):
    i = pl.program_id(0)
    hn = _rms(h_ref[...], g_ref[...]).astype(bf16)
    p = _dot(hn, win_ref[...])
    for hd in range(HEADS):
        nq_out[hd] = p[:, 512 + 64 * hd:576 + 64 * hd].astype(bf16)
        nk_out[hd] = p[:, 1024 + 64 * hd:1088 + 64 * hd].astype(bf16)
        nv_out[hd] = p[:, 1536 + 64 * hd:1600 + 64 * hd].astype(bf16)
    cqn = _rms(p[:, 0:256], qn_ref[...]).astype(bf16)
    ckvn = _rms(p[:, 256:384], kvn_ref[...]).astype(bf16)
    kpe = p[:, 384:512].astype(bf16)
    qa = _dot(cqn, wq_ref[...])
    qr = _dot(cqn, wqr_ref[...])
    ka = _dot(ckvn, wk_ref[...]) + _dot(kpe, e_ref[...])
    kr = _dot(kpe, er_ref[...])
    va = _dot(ckvn, wv_ref[...])
    cos = cos_ref[...]
    sin = sin_ref[...]
    t = i * TM + lax.broadcasted_iota(jnp.int32, (TM, 1), 0)
    lane = lax.broadcasted_iota(jnp.int32, (1, 128), 1)
    ones_col = jnp.where((t < T) & (lane == 64), 1.0, 0.0)
    for hd in range(HEADS):
        sl = slice(hd * 128, (hd + 1) * 128)
        qt_out[sl, :] = (qa[:, sl] * cos + qr[:, sl] * sin).T.astype(bf16)
        k_out[hd] = (ka[:, sl] * cos + kr[:, sl] * sin).astype(bf16)
        vt_out[hd, 0] = (va[:, sl] + ones_col).T[:VT_ROWS].astype(bf16)


def _in0(h, g, w):
    full = lambda shape: pl.BlockSpec(shape, lambda i: (0,) * len(shape))
    row = lambda n: pl.BlockSpec((TM, n), lambda i: (i, 0))
    hm = lambda n: pl.BlockSpec((HEADS, TM, n), lambda i: (0, i, 0))
    outs = [jax.ShapeDtypeStruct((HEADS * 128, TP), bf16), jax.ShapeDtypeStruct((HEADS, TP, 128), bf16),
            jax.ShapeDtypeStruct((HEADS, NB, VT_ROWS, TM), bf16)] + [jax.ShapeDtypeStruct((HEADS, TP, 64), bf16)] * 3
    return pl.pallas_call(
        _k_in0,
        grid=(NB,),
        in_specs=[row(1024), full((1, 1024)), full((1024, 2048)), full((1, 256)), full((1, 128)),
                  full((256, 1024)), full((256, 1024)), full((128, 1024)), full((128, 1024)), full((128, 1024)),
                  full((128, 1024)), row(128), row(128)],
        out_specs=[pl.BlockSpec((HEADS * 128, TM), lambda i: (0, i)), hm(128),
                   pl.BlockSpec((HEADS, 1, VT_ROWS, TM), lambda i: (0, i, 0, 0)), hm(64), hm(64), hm(64)],
        out_shape=outs,
        compiler_params=_cp(("parallel",)),
        name="in0",
    )(h, g, w["win"], w["qn"], w["kvn"], w["wq"], w["wqr"], w["wk"], w["e"], w["er"], w["wv"], w["cos"], w["sin"])


def _k_mla(qt_ref, k_ref, vt_ref, o_ref, sa_ref, sb_ref, m_ref, acc_ref):
    qt = qt_ref[...]
    m_ref[...] = jnp.full((1, TM), NEG, f32)
    acc_ref[...] = jnp.zeros((VT_ROWS, TM), f32)

    def scores(j):
        return _dot(k_ref[0, j], qt)

    def update(s_ref, j, masked):
        s = s_ref[...]
        if masked:
            key = lax.broadcasted_iota(jnp.int32, (TM, 1), 0)
            s = jnp.where(key < T - (NB - 1) * TM, s, NEG)
        m = m_ref[...]
        m_new = jnp.maximum(m, s.max(axis=0, keepdims=True))
        p = jnp.exp2(s - m_new).astype(bf16)
        acc_ref[...] = jnp.exp2(m - m_new) * acc_ref[...] + _dot(vt_ref[0, j], p)
        m_ref[...] = m_new

    sa_ref[...] = scores(0)

    def group(i, _):
        for u in range(0, MLA_UNROLL, 2):
            j = MLA_UNROLL * i + u
            sb_ref[...] = scores(j + 1)
            update(sa_ref, j, False)
            sa_ref[...] = scores(j + 2)
            update(sb_ref, j + 1, False)
        return 0

    lax.fori_loop(0, (NB - 1) // MLA_UNROLL, group, 0)
    update(sa_ref, NB - 1, True)
    acc = acc_ref[...]
    o_ref[0] = (acc[:64] / acc[64:65]).astype(bf16)


def _mla(qt, k, vt):
    return pl.pallas_call(
        _k_mla,
        grid=(HEADS, NB),
        in_specs=[pl.BlockSpec((128, TM), lambda h, i: (h, i)),
                  pl.BlockSpec((1, NB, TM, 128), lambda h, i: (h, 0, 0, 0)),
                  pl.BlockSpec((1, NB, VT_ROWS, TM), lambda h, i: (h, 0, 0, 0))],
        out_specs=pl.BlockSpec((1, 64, TM), lambda h, i: (h, 0, i)),
        out_shape=jax.ShapeDtypeStruct((HEADS, 64, TP), bf16),
        scratch_shapes=[pltpu.VMEM((TM, TM), f32), pltpu.VMEM((TM, TM), f32), pltpu.VMEM((1, TM), f32),
                        pltpu.VMEM((VT_ROWS, TM), f32)],
        compiler_params=_cp(("parallel", "parallel")),
        name="mla",
    )(qt, k.reshape(HEADS, NB, TM, 128), vt)


def _k_nat(q_ref, k_ref, v_ref, c_ref, o_ref):
    km = k_ref[0, 0:N_META, :]
    vm = v_ref[0, 0:N_META, :]
    qm = q_ref[0, 0:N_META, :]
    s = lax.dot_general(qm, km, (((1,), (1,)), ((), ())), preferred_element_type=f32)
    p = jnp.exp(s - s.max(axis=1, keepdims=True))
    o_ref[0, 0:N_META, :] = (_dot(p.astype(bf16), vm) / p.sum(axis=1, keepdims=True)).astype(bf16)
    o_ref[0, T:TP, :] = jnp.zeros((TP - T, 64), bf16)

    def rows(ii, _):
        idx = [ii * NAT_UNROLL + r for r in range(NAT_UNROLL)]
        r0s = [jnp.clip(i - WIN_ROWS // 2, 0, ROWS - WIN_ROWS) for i in idx]
        qoffs = [pl.multiple_of(N_META + GRID_W * i, 16) for i in idx]
        koffs = [pl.multiple_of(N_META + GRID_W * r0, 16) for r0 in r0s]
        ss, sms = [], []
        for i, r0, qoff, koff in zip(idx, r0s, qoffs, koffs):
            a0 = r0 - i + WIN_ROWS - 1
            qi = q_ref[0, pl.ds(qoff, GRID_W), :]
            kw = k_ref[0, pl.ds(koff, WIN_ROWS * GRID_W), :]
            s = lax.dot_general(qi, kw, (((1,), (1,)), ((), ())), preferred_element_type=f32)
            ss.append(s + jnp.concatenate([c_ref[0, a0 + 2 * j] for j in range(WIN_ROWS // 2)], axis=1))
            sms.append(lax.dot_general(qi, km, (((1,), (1,)), ((), ())), preferred_element_type=f32))
        ps, pms, ls = [], [], []
        for s, sm in zip(ss, sms):
            m = jnp.maximum(s.max(axis=1, keepdims=True), sm.max(axis=1, keepdims=True))
            p = jnp.exp(s - m)
            pm = jnp.exp(sm - m)
            ls.append(p.sum(axis=1, keepdims=True) + pm.sum(axis=1, keepdims=True))
            ps.append(p.astype(bf16))
            pms.append(pm.astype(bf16))
        for p, pm, l, qoff, koff in zip(ps, pms, ls, qoffs, koffs):
            vw = v_ref[0, pl.ds(koff, WIN_ROWS * GRID_W), :]
            o = _dot(p, vw) + _dot(pm, vm)
            o_ref[0, pl.ds(qoff, GRID_W), :] = (o / l).astype(bf16)
        return 0

    lax.fori_loop(0, ROWS // NAT_UNROLL, rows, 0)


def _nat(q, k, v, c2):
    blk = pl.BlockSpec((1, TP, 64), lambda h: (h, 0, 0))
    return pl.pallas_call(
        _k_nat,
        grid=(HEADS,),
        in_specs=[blk, blk, blk, pl.BlockSpec((1, 2 * WIN_ROWS - 2, GRID_W, 2 * GRID_W), lambda h: (h, 0, 0, 0))],
        out_specs=blk,
        out_shape=jax.ShapeDtypeStruct((HEADS, TP, 64), bf16),
        compiler_params=_cp(("parallel",)),
        name="nat",
    )(q, k, v, c2)


def _k_out_router(h_ref, a_ref, b_ref, wa_ref, wb_ref, g_ref, wr_ref, br_ref, h_out, xg_out, gidx_out, *, a_transposed):
    if a_transposed:
        ya = lax.dot_general(a_ref[...], wa_ref[...], (((0,), (0,)), ((), ())), preferred_element_type=f32)
    else:
        ya = _dot(a_ref[...].reshape(TM, 512).astype(bf16), wa_ref[...])
    h1 = h_ref[...] + ya + _dot(b_ref[...].astype(bf16), wb_ref[...])
    h_out[...] = h1
    xn = _rms(h1, g_ref[...])
    xg_out[:, :D_MODEL] = xn
    xh = xn.astype(bf16)
    xl = (xn - xh.astype(f32)).astype(bf16)
    logits = _dot(xh, wr_ref[0]) + _dot(xh, wr_ref[1]) + _dot(xl, wr_ref[0]) + br_ref[...]
    lane = lax.broadcasted_iota(jnp.int32, logits.shape, 1)
    big = jnp.int32(1 << 20)
    gl = jnp.where((lane >= 32) & (lane < 36), logits, NEG)
    gmax = gl.max(axis=1, keepdims=True)
    gsum = jnp.exp(gl - gmax).sum(axis=1, keepdims=True)
    p_group = 1.0 / gsum
    gidx = jnp.clip(jnp.where(gl == gmax, lane, big).min(axis=1, keepdims=True) - 32, 0, 3)
    el = jnp.where((lane >> 3) == gidx, logits, NEG)
    v1 = el.max(axis=1, keepdims=True)
    i1 = jnp.where(el == v1, lane, big).min(axis=1, keepdims=True)
    el2 = jnp.where(lane == i1, NEG, el)
    v2 = el2.max(axis=1, keepdims=True)
    i2 = jnp.where(el2 == v2, lane, big).min(axis=1, keepdims=True)
    e2 = jnp.exp(v2 - v1)
    w1 = p_group / (1.0 + e2)
    w2 = w1 * e2
    xg_out[:, D_MODEL:] = jnp.where(lane == i1, w1, jnp.where(lane == i2, w2, 0.0))
    gidx_out[...] = gidx


def _out_router(h, a, b, wa, wb, g, wr, br, a_transposed=False):
    full = lambda shape: pl.BlockSpec(shape, lambda i: (0,) * len(shape))
    row = lambda n: pl.BlockSpec((TM, n), lambda i: (i, 0))
    a_spec = (pl.BlockSpec((512, TM), lambda i: (0, i)) if a_transposed
              else pl.BlockSpec((TM // N2, N2, 512), lambda i: (i, 0, 0)))
    return pl.pallas_call(
        functools.partial(_k_out_router, a_transposed=a_transposed),
        grid=(NB,),
        in_specs=[row(1024), a_spec, row(512), full((512, 1024)), full((512, 1024)), full((1, 1024)),
                  full((2, 1024, 128)), full((1, 128))],
        out_specs=[row(1024), row(XG_W), row(1)],
        out_shape=[jax.ShapeDtypeStruct((TP, 1024), f32), jax.ShapeDtypeStruct((TP, XG_W), f32),
                   jax.ShapeDtypeStruct((TP, 1), jnp.int32)],
        compiler_params=_cp(("parallel",)),
        name="out_router",
    )(h, a, b, wa, wb, g, wr, br)


def _k_rank(g_ref, rank_out, cnt_out, carry_ref):
    @pl.when(pl.program_id(0) == 0)
    def _():
        carry_ref[...] = jnp.zeros((1, 128), f32)

    lane = lax.broadcasted_iota(jnp.int32, (TM, 128), 1)
    oh = jnp.where(lane == g_ref[...], 1.0, 0.0)
    r = lax.broadcasted_iota(jnp.int32, (TM, TM), 0)
    c = lax.broadcasted_iota(jnp.int32, (TM, TM), 1)
    earlier = jnp.where(c < r, 1.0, 0.0).astype(bf16)
    excl = _dot(earlier, oh.astype(bf16)) + carry_ref[...]
    rank_out[...] = (oh * excl).sum(axis=1, keepdims=True).astype(jnp.int32)
    carry_ref[...] += oh.sum(axis=0, keepdims=True)
    cnt_out[...] = carry_ref[...]


def _rank(gidx):
    return pl.pallas_call(
        _k_rank,
        grid=(NB,),
        in_specs=[pl.BlockSpec((TM, 1), lambda i: (i, 0))],
        out_specs=[pl.BlockSpec((TM, 1), lambda i: (i, 0)), pl.BlockSpec((1, 128), lambda i: (0, 0))],
        out_shape=[jax.ShapeDtypeStruct((TP, 1), jnp.int32), jax.ShapeDtypeStruct((1, 128), f32)],
        scratch_shapes=[pltpu.VMEM((1, 128), f32)],
        compiler_params=_cp(("arbitrary",)),
        name="moe_rank",
    )(gidx)


def _row_copy(src_ref, src_row, dst_ref, dst_row, sem):
    return pltpu.make_async_copy(src_ref.at[pl.ds(src_row, 1)], dst_ref.at[pl.ds(dst_row, 1)], sem)


def _k_dispatch(pos_ref, x_ref, xs_in_ref, xs_ref, sem):
    del xs_in_ref
    base = pl.program_id(0) * TM

    def start(r, _):
        _row_copy(x_ref, r, xs_ref, pos_ref[base + r], sem).start()
        return 0

    lax.fori_loop(0, TM, start, 0, unroll=8)
    pltpu.make_async_copy(x_ref, xs_ref.at[pl.ds(0, TM)], sem).wait()


def _dispatch(pos, xg):
    return pl.pallas_call(
        _k_dispatch,
        grid_spec=pltpu.PrefetchScalarGridSpec(
            num_scalar_prefetch=1, grid=(NB,),
            in_specs=[pl.BlockSpec((TM, XG_W), lambda i, pos: (i, 0)), pl.BlockSpec(memory_space=pl.ANY)],
            out_specs=pl.BlockSpec(memory_space=pl.ANY),
            scratch_shapes=[pltpu.SemaphoreType.DMA(())]),
        out_shape=jax.ShapeDtypeStruct((SP, XG_W), f32),
        input_output_aliases={2: 0},
        compiler_params=_cp(("arbitrary",)),
        name="moe_dispatch",
    )(pos, xg, jnp.zeros((SP, XG_W), f32))


def _k_moe(tg_ref, nt_ref, xs_ref, w13_ref, w2_ref, ex_ref, ys_ref):
    del tg_ref
    live = pl.program_id(0) < nt_ref[0]
    half = pl.program_id(1)

    @pl.when(jnp.logical_not(live) & (half == 0))
    def _():
        ys_ref[...] = jnp.zeros((SORT_TILE, D_MODEL), f32)

    @pl.when(live)
    def _():
        x = xs_ref[...]
        a = _dot(x[:, :D_MODEL].astype(bf16), w13_ref[0, 0])
        gate = x[:, D_MODEL:]
        ghi = gate.astype(bf16)
        glo = (gate - ghi.astype(f32)).astype(bf16)
        gx = _dot(ghi, ex_ref[0, 0]) + _dot(glo, ex_ref[0, 0])
        a1 = a[:, :1024]
        hid = a1 * jax.nn.sigmoid(a1) * a[:, 1024:] * gx
        y = _dot(hid.astype(bf16), w2_ref[0, 0])

        @pl.when(half == 0)
        def _():
            ys_ref[...] = y

        @pl.when(half == 1)
        def _():
            ys_ref[...] += y


def _moe_sorted(tile_group, n_tiles, xs, w13, w2, ex):
    wspec = lambda shape: pl.BlockSpec((1, 1) + shape, lambda j, s, tg, nt: (tg[j], s, 0, 0))
    return pl.pallas_call(
        _k_moe,
        grid_spec=pltpu.PrefetchScalarGridSpec(
            num_scalar_prefetch=2, grid=(SP // SORT_TILE, 2),
            in_specs=[pl.BlockSpec((SORT_TILE, XG_W), lambda j, s, tg, nt: (j, 0)),
                      wspec((1024, 2048)), wspec((1024, 1024)), wspec((128, 1024))],
            out_specs=pl.BlockSpec((SORT_TILE, D_MODEL), lambda j, s, tg, nt: (j, 0))),
        out_shape=jax.ShapeDtypeStruct((SP, D_MODEL), f32),
        compiler_params=_cp(("arbitrary", "arbitrary")),
        name="moe",
    )(tile_group, n_tiles, xs, w13, w2, ex)


def _k_combine(pos_ref, h_ref, ys_ref, o_ref, buf_ref, sem):
    base = pl.program_id(0) * TM

    def start(r, _):
        _row_copy(ys_ref, pos_ref[base + r], buf_ref, r, sem).start()
        return 0

    lax.fori_loop(0, TM, start, 0, unroll=8)
    pltpu.make_async_copy(ys_ref.at[pl.ds(0, TM)], buf_ref, sem).wait()
    o_ref[...] = h_ref[...] + buf_ref[...]


def _combine(pos, h, ys):
    return pl.pallas_call(
        _k_combine,
        grid_spec=pltpu.PrefetchScalarGridSpec(
            num_scalar_prefetch=1, grid=(NB,),
            in_specs=[pl.BlockSpec((TM, D_MODEL), lambda i, pos: (i, 0)), pl.BlockSpec(memory_space=pl.ANY)],
            out_specs=pl.BlockSpec((TM, D_MODEL), lambda i, pos: (i, 0)),
            scratch_shapes=[pltpu.VMEM((TM, D_MODEL), f32), pltpu.SemaphoreType.DMA(())]),
        out_shape=jax.ShapeDtypeStruct((TP, D_MODEL), f32),
        compiler_params=_cp(("arbitrary",)),
        name="moe_combine",
    )(pos, h, ys)


def _moe(xg, gidx, h, w13, w2, ex):
    rank, cnt = _rank(gidx)
    cnt = cnt[0, :4].astype(jnp.int32)
    tiles = (cnt + SORT_TILE - 1) // SORT_TILE
    tile_end = jnp.cumsum(tiles)
    start = (tile_end - tiles) * SORT_TILE
    pos = start[gidx[:, 0]] + rank[:, 0]
    tile_group = jnp.minimum(jnp.sum(jnp.arange(SP // SORT_TILE)[:, None] >= tile_end[None, :], axis=1), 3)
    xs = _dispatch(pos, xg)
    ys = _moe_sorted(tile_group.astype(jnp.int32), tile_end[3:4].astype(jnp.int32), xs, w13, w2, ex)
    return _combine(pos, h, ys)


def _k_in1(h_ref, g_ref, w_ref, hy_out, cf_out):
    hn = _rms(h_ref[...], g_ref[...]).astype(bf16)
    p = _dot(hn, w_ref[...])
    hy_out[...] = p[:, :1536]
    cf_out[...] = p[:, 1536:]


def _in1(h, g, w):
    row = lambda n: pl.BlockSpec((TM, n), lambda i: (i, 0))
    return pl.pallas_call(
        _k_in1,
        grid=(NB,),
        in_specs=[row(1024), pl.BlockSpec((1, 1024), lambda i: (0, 0)), pl.BlockSpec((1024, 2560), lambda i: (0, 0))],
        out_specs=[row(1536), row(1024)],
        out_shape=[jax.ShapeDtypeStruct((TP, 1536), f32), jax.ShapeDtypeStruct((TP, 1024), f32)],
        compiler_params=_cp(("parallel",)),
        name="in1",
    )(h, g, w)


def _k_short(prev_ref, cur_ref, next_ref, w_ref, b_ref, v_out, x1_out, x2_out):
    i = pl.program_id(0)
    ext = jnp.concatenate([prev_ref[...], cur_ref[...], next_ref[...]], axis=0)
    t = i * TM - 8 + lax.broadcasted_iota(jnp.int32, (TM + 16, 1), 0)
    ext = jnp.where((t >= 0) & (t < T), ext, 0.0)
    w = w_ref[...]
    s = ext[7:7 + TM] * w[0:1] + ext[8:8 + TM] * w[1:2] + ext[9:9 + TM] * w[2:3] + b_ref[...]
    s = jnp.where(t[8:8 + TM] < T, s, 0.0)
    v_out[...] = s[:, :512].reshape(TM // N2, N2, 512)
    x1_out[...] = s[:, 512:1024].reshape(TM // N2, N2, 512)
    x2_out[...] = s[:, 1024:].reshape(TM // N2, N2, 512)


def _short(hy, w, b):
    last = NB - 1
    nblk = THY // TM
    return pl.pallas_call(
        _k_short,
        grid=(nblk,),
        in_specs=[pl.BlockSpec((8, 1536), lambda i: (jnp.maximum(jnp.minimum(i, last) * (TM // 8) - 1, 0), 0)),
                  pl.BlockSpec((TM, 1536), lambda i: (jnp.minimum(i, last), 0)),
                  pl.BlockSpec((8, 1536), lambda i: (jnp.minimum((jnp.minimum(i, last) + 1) * (TM // 8), TP // 8 - 1), 0)),
                  pl.BlockSpec((3, 1536), lambda i: (0, 0)), pl.BlockSpec((1, 1536), lambda i: (0, 0))],
        out_specs=[pl.BlockSpec((TM // N2, N2, 512), lambda i: (i, 0, 0))] * 3,
        out_shape=[jax.ShapeDtypeStruct((N1P, N2, 512), f32)] * 3,
        compiler_params=_cp(("parallel",)),
        name="hy_short",
    )(hy, hy, hy, w, b)


def _k_filt(feat_ref, w1_ref, b1_ref, w2_ref, b2_ref, w3_ref, fr_ref, rate_ref, o_ref):
    i = pl.program_id(0)
    feats = feat_ref[...]
    fr = fr_ref[...]
    hid = jnp.sin(fr * (_dot_hi(feats, w1_ref[...]) + b1_ref[...]))
    hid = jnp.sin(fr * (_dot_hi(hid, w2_ref[...]) + b2_ref[...]))
    filt = _dot(hid.astype(bf16), w3_ref[...].astype(bf16))
    filt = filt * jnp.exp(-feats[:, 0:1] * rate_ref[...])
    lag = i * TM + lax.broadcasted_iota(jnp.int32, (TM, 1), 0)
    col = lax.broadcasted_iota(jnp.int32, (1, 2048), 1)
    keep = (lag < T) & ((lag > 0) | (col < 1024))
    o_ref[...] = jnp.where(keep, filt, 0.0).reshape(TM // N2, N2, 2048)


def _filt(feats, w1, b1, w2, b2, w3, fr, rate):
    full = lambda shape: pl.BlockSpec(shape, lambda i: (0,) * len(shape))
    return pl.pallas_call(
        _k_filt,
        grid=(THY // TM,),
        in_specs=[pl.BlockSpec((TM, 128), lambda i: (i, 0)), full((128, 64)), full((1, 64)), full((64, 64)),
                  full((1, 64)), full((64, 2048)), full((1, 64)), full((1, 2048))],
        out_specs=pl.BlockSpec((TM // N2, N2, 2048), lambda i: (i, 0, 0)),
        out_shape=jax.ShapeDtypeStruct((N1P, N2, 2048), f32),
        compiler_params=_cp(("parallel",)),
        name="hy_filt",
    )(feats, w1, b1, w2, b2, w3, fr, rate)


def _k_dft1(f_ref, x_ref, y_ref):
    f = f_ref[...]
    for j in range(8):
        y = _dot(f, x_ref[:, j, :].astype(bf16))
        y_ref[:, 0, j, :] = y[:K1]
        y_ref[:, 1, j, :] = y[K1:]


def _dft1(f1c, x3):
    c = x3.shape[2]
    return pl.pallas_call(
        _k_dft1,
        grid=(N2 // 8, c // 512),
        in_specs=[pl.BlockSpec((2 * K1, N1P), lambda i, cb: (0, 0)),
                  pl.BlockSpec((N1P, 8, 512), lambda i, cb: (0, i, cb))],
        out_specs=pl.BlockSpec((K1, 2, 8, 512), lambda i, cb: (0, 0, i, cb)),
        out_shape=jax.ShapeDtypeStruct((K1, 2, N2, c), f32),
        compiler_params=_cp(("parallel", "parallel")),
        name="dft1",
    )(f1c, x3)


def _stage2_matrix(f2_ref, tw_ref):
    fr, fi = f2_ref[0], f2_ref[1]
    tr, ti = tw_ref[0, 0:1, :], tw_ref[0, 1:2, :]
    gre = fr * tr - fi * ti
    gim = fr * ti + fi * tr
    return jnp.concatenate([jnp.concatenate([gre, -gim], axis=1), jnp.concatenate([gim, gre], axis=1)], axis=0)


def _k_spec_filt(y_ref, f2_ref, tw_ref, o_ref):
    y = y_ref[0].reshape(2 * N2, 2048).astype(bf16)
    x = _dot(_stage2_matrix(f2_ref, tw_ref).astype(bf16), y)
    o_ref[0, 0] = x[:N2, :1024] + x[:N2, 1024:]
    o_ref[0, 1] = x[N2:, :1024] - x[N2:, 1024:]


def _spec_filt(yf, f2, tw):
    return pl.pallas_call(
        _k_spec_filt,
        grid=(K1,),
        in_specs=[pl.BlockSpec((1, 2, N2, 2048), lambda k: (k, 0, 0, 0)),
                  pl.BlockSpec((2, N2, N2), lambda k: (0, 0, 0)), pl.BlockSpec((1, 2, N2), lambda k: (k, 0, 0))],
        out_specs=pl.BlockSpec((1, 2, N2, 1024), lambda k: (k, 0, 0, 0)),
        out_shape=jax.ShapeDtypeStruct((K1, 2, N2, 1024), f32),
        compiler_params=_cp(("parallel",)),
        name="spec_filt",
    )(yf, f2, tw)


def _k_spec_conv(y_ref, kf_ref, f2_ref, tw_ref, z_ref):
    g = _stage2_matrix(f2_ref, tw_ref)
    y = y_ref[0].reshape(2 * N2, 512).astype(bf16)
    x = _dot(g.astype(bf16), y)
    xr, xi = x[:N2], x[N2:]
    kr, ki = kf_ref[0, 0], kf_ref[0, 1]
    p = jnp.concatenate([xr * kr - xi * ki, xr * ki + xi * kr], axis=0).astype(bf16)
    z_ref[0] = _dot(g.T.astype(bf16), p).reshape(2, N2, 512)


def _spec_conv(y, kf, order, f2, tw):
    return pl.pallas_call(
        _k_spec_conv,
        grid=(K1,),
        in_specs=[pl.BlockSpec((1, 2, N2, 512), lambda k: (k, 0, 0, 0)),
                  pl.BlockSpec((1, 2, N2, 512), lambda k: (k, 0, 0, order)),
                  pl.BlockSpec((2, N2, N2), lambda k: (0, 0, 0)), pl.BlockSpec((1, 2, N2), lambda k: (k, 0, 0))],
        out_specs=pl.BlockSpec((1, 2, N2, 512), lambda k: (k, 0, 0, 0)),
        out_shape=jax.ShapeDtypeStruct((K1, 2, N2, 512), f32),
        compiler_params=_cp(("parallel",)),
        name="spec_conv",
    )(y, kf, f2, tw)


def _k_idft(f_ref, z_ref, zin_ref, gate_ref, bias_ref, o_ref):
    f = f_ref[...]
    bias = bias_ref[...]
    for j in range(8):
        zj = jnp.concatenate([z_ref[:, 0, j, :], z_ref[:, 1, j, :]], axis=0).astype(bf16)
        y = _dot(f, zj)
        o_ref[:, j, :] = gate_ref[:, j, :] * (y + zin_ref[:, j, :] * bias)


def _idft(f1inv, z, zin, gate, bias):
    blk = pl.BlockSpec((N1P, 8, 512), lambda i: (0, i, 0))
    return pl.pallas_call(
        _k_idft,
        grid=(N2 // 8,),
        in_specs=[pl.BlockSpec((N1P, 2 * K1), lambda i: (0, 0)),
                  pl.BlockSpec((K1, 2, 8, 512), lambda i: (0, 0, i, 0)), blk, blk,
                  pl.BlockSpec((1, 512), lambda i: (0, 0))],
        out_specs=blk,
        out_shape=jax.ShapeDtypeStruct((N1P, N2, 512), f32),
        compiler_params=_cp(("parallel",)),
        name="idft",
    )(f1inv, z, zin, gate, bias)


def _k_conf(prev_ref, cur_ref, next_ref, w_ref, b_ref, lg_ref, lb_ref, o_ref):
    i = pl.program_id(0)
    ext = jnp.concatenate([prev_ref[...], cur_ref[...], next_ref[...]], axis=0)
    t = i * TM - 16 + lax.broadcasted_iota(jnp.int32, (TM + 32, 1), 0)
    u = ext[:, :512] * jax.nn.sigmoid(ext[:, 512:])
    u = jnp.where((t >= 0) & (t < T), u, 0.0)
    w = w_ref[...]
    acc = jnp.zeros((TM, 512), f32) + b_ref[...]
    for j in range(CF_K):
        acc = acc + u[j + 1:j + 1 + TM] * w[j:j + 1]
    mu = jnp.mean(acc, axis=-1, keepdims=True)
    xc = acc - mu
    y = xc * lax.rsqrt(jnp.mean(xc * xc, axis=-1, keepdims=True) + EPS) * lg_ref[...] + lb_ref[...]
    o_ref[...] = (y * jax.nn.sigmoid(y)).astype(bf16)


def _conf(cf, w, b, lg, lb):
    full = lambda shape: pl.BlockSpec(shape, lambda i: (0,) * len(shape))
    return pl.pallas_call(
        _k_conf,
        grid=(NB,),
        in_specs=[pl.BlockSpec((16, 1024), lambda i: (jnp.maximum(i * (TM // 16) - 1, 0), 0)),
                  pl.BlockSpec((TM, 1024), lambda i: (i, 0)),
                  pl.BlockSpec((16, 1024), lambda i: (jnp.minimum((i + 1) * (TM // 16), TP // 16 - 1), 0)),
                  full((CF_K, 512)), full((1, 512)), full((1, 512)), full((1, 512))],
        out_specs=pl.BlockSpec((TM, 512), lambda i: (i, 0)),
        out_shape=jax.ShapeDtypeStruct((TP, 512), bf16),
        compiler_params=_cp(("parallel",)),
        name="conformer",
    )(cf, cf, cf, w, b, lg, lb)


def _k_final(h_ref, nxt_ref, g_ref, o_ref):
    h = jnp.concatenate([h_ref[N_META:, :], nxt_ref[...]], axis=0)
    o_ref[...] = _rms(h, g_ref[...])


def _final(h, g):
    return pl.pallas_call(
        _k_final,
        grid=(SEQ // TM,),
        in_specs=[pl.BlockSpec((TM, 1024), lambda i: (i, 0)),
                  pl.BlockSpec((N_META, 1024), lambda i: ((i + 1) * (TM // N_META), 0)),
                  pl.BlockSpec((1, 1024), lambda i: (0, 0))],
        out_specs=pl.BlockSpec((TM, 1024), lambda i: (i, 0)),
        out_shape=jax.ShapeDtypeStruct((SEQ, 1024), f32),
        compiler_params=_cp(("parallel",)),
        name="final_norm",
    )(h, h, g)


def _prep_in0(a_w_in, q_norm, w_uq, kv_norm, w_ukv):
    win = jnp.concatenate([a_w_in[:, :416], jnp.zeros((D_MODEL, 96), f32), a_w_in[:, 416:928] * (64 ** -0.5),
                           a_w_in[:, 928:]], axis=1).astype(bf16)
    wq3 = (w_uq * (MLA_DK ** -0.5 * math.log2(math.e))).reshape(256, HEADS, MLA_DK)
    rope = wq3[..., 64:]
    rot = jnp.concatenate([-rope[..., 16:], rope[..., :16]], axis=-1)
    z64 = jnp.zeros((256, HEADS, 64), f32)
    z32 = jnp.zeros((256, HEADS, 32), f32)
    wq = jnp.concatenate([wq3, z32], axis=-1).reshape(256, 1024).astype(bf16)
    wqr = jnp.concatenate([z64, rot, z32], axis=-1).reshape(256, 1024).astype(bf16)
    kv3 = w_ukv.reshape(128, HEADS, 128)
    wk = jnp.concatenate([kv3[..., :64], jnp.zeros((128, HEADS, 64), f32)], axis=-1).reshape(128, 1024).astype(bf16)
    wv = jnp.concatenate([kv3[..., 64:], jnp.zeros((128, HEADS, 64), f32)], axis=-1).reshape(128, 1024).astype(bf16)
    e = np.zeros((128, HEADS, 128), np.float32)
    er = np.zeros((128, HEADS, 128), np.float32)
    for r in range(32):
        e[r, :, 64 + r] = 1.0
        if r < 16:
            er[r + 16, :, 64 + r] = -1.0
        else:
            er[r - 16, :, 64 + r] = 1.0
    pos = jnp.arange(TP, dtype=f32)
    inv_freq = 10000.0 ** (-jnp.arange(0, 32, 2, dtype=f32) / 32)
    ang = pos[:, None] * inv_freq[None, :]
    c, s = jnp.cos(ang), jnp.sin(ang)
    cos = jnp.concatenate([jnp.ones((TP, 64), f32), c, c, jnp.zeros((TP, 32), f32)], axis=1)
    sin = jnp.concatenate([jnp.zeros((TP, 64), f32), s, s, jnp.zeros((TP, 32), f32)], axis=1)
    return dict(win=win, qn=q_norm[None], kvn=kv_norm[None], wq=wq, wqr=wqr, wk=wk,
                e=jnp.asarray(e.reshape(128, 1024)).astype(bf16), er=jnp.asarray(er.reshape(128, 1024)).astype(bf16),
                wv=wv, cos=cos, sin=sin)


def _nat_bias(rpb):
    qc = np.arange(GRID_W)[:, None]
    kc = np.arange(GRID_W)[None, :]
    c0 = np.clip(qc - WIN_COLS // 2, 0, GRID_W - WIN_COLS)
    inside = (kc >= c0) & (kc < c0 + WIN_COLS)
    idx = np.clip(kc - qc + WIN_COLS - 1, 0, 2 * WIN_COLS - 2)
    c = jnp.where(jnp.asarray(inside)[None, None], rpb[:, :, idx], NEG)
    return jnp.concatenate([c[:, :-1], c[:, 1:]], axis=-1)


def _prep_moe(w_group, b_group, w_router, b_router, w1, w3, w2):
    wr = jnp.concatenate([w_router.transpose(1, 0, 2).reshape(D_MODEL, 32), w_group,
                          jnp.zeros((D_MODEL, 92), f32)], axis=1)
    br = jnp.concatenate([b_router.reshape(32), b_group, jnp.zeros((92,), f32)])[None]
    wr_hi = wr.astype(bf16)
    wr = jnp.stack([wr_hi, (wr - wr_hi.astype(f32)).astype(bf16)])

    def half(w):
        return w.reshape(4, 2, 4, D_MODEL, 256).transpose(0, 1, 3, 2, 4).reshape(4, 2, D_MODEL, 1024)

    w13 = jnp.concatenate([half(w1), half(w3)], axis=-1).astype(bf16)
    w2h = w2.reshape(4, 2, 1024, D_MODEL).astype(bf16)
    ex = np.zeros((4, 2, 128, 4, 256), np.float32)
    for g in range(4):
        for s in range(2):
            for e in range(4):
                ex[g, s, g * 8 + s * 4 + e, e, :] = 1.0
    return wr, br, w13, w2h, jnp.asarray(ex.reshape(4, 2, 128, 1024)).astype(bf16)


def _dft_tables():
    two_pi = 2.0 * math.pi
    k1 = jnp.arange(K1, dtype=jnp.int32)
    n1 = jnp.arange(N1P, dtype=jnp.int32)
    th = two_pi * ((k1[:, None] * n1[None, :]) % N1).astype(f32) / N1
    f1c = jnp.concatenate([jnp.cos(th), -jnp.sin(th)], axis=0).astype(bf16)
    wgt = jnp.where((k1 == 0) | (k1 == N1 // 2), 1.0, 2.0).astype(f32) / NFFT
    f1inv = jnp.concatenate([jnp.cos(th).T * wgt[None, :], -jnp.sin(th).T * wgt[None, :]], axis=1).astype(bf16)
    n2 = jnp.arange(N2, dtype=jnp.int32)
    a2 = -two_pi * ((n2[:, None] * n2[None, :]) % N2).astype(f32) / N2
    f2 = jnp.stack([jnp.cos(a2), jnp.sin(a2)])
    at = -two_pi * ((k1[:, None] * n2[None, :]) % NFFT).astype(f32) / NFFT
    tw = jnp.stack([jnp.cos(at), jnp.sin(at)], axis=1)
    return f1c, f1inv, f2, tw


def _hyena(hy_in, short_w, short_b, f_w1, f_b1, f_w2, f_b2, f_w3, sin_freq, log_decay, hy_bias):
    v, x1, x2 = _short(hy_in, short_w, short_b[None])
    f1c, f1inv, f2, tw = _dft_tables()
    tt = jnp.linspace(0.0, 1.0, T, dtype=f32)
    bands = jnp.linspace(1e-4, 15.0, 16, dtype=f32)
    ang = 2.0 * math.pi * tt[:, None] * bands[None, :]
    feats = jnp.concatenate([tt[:, None], jnp.cos(ang), jnp.sin(ang), jnp.zeros((T, 95), f32)], axis=-1)
    feats = jnp.concatenate([feats, jnp.zeros((THY - T, 128), f32)], axis=0)
    w1p = jnp.concatenate([f_w1, jnp.zeros((95, 64), f32)], axis=0)
    rate = jnp.exp(log_decay.astype(f32)).reshape(1, 2048)
    filt = _filt(feats, w1p, f_b1[None], f_w2, f_b2[None], f_w3, sin_freq[None], rate)
    kf = _spec_filt(_dft1(f1c, filt), f2, tw)
    z = v
    for o, gate in enumerate((x1, x2)):
        zf = _spec_conv(_dft1(f1c, z), kf, o, f2, tw)
        z = _idft(f1inv, zf, z, gate, hy_bias[o][None])
    return z


def kernel(x, meta_tokens, norm_mix, norm_ffn, norm_final, a_w_in, mla_q_norm, mla_w_uq, mla_kv_norm, mla_w_ukv, nat_rpb, a_w_out, c_w_in, hy_short_w, hy_short_b, hy_ffn_w1, hy_ffn_b1, hy_ffn_w2, hy_ffn_b2, hy_ffn_w3, hy_sin_freq, hy_log_decay, hy_bias, cf_dw_w, cf_dw_b, cf_ln_g, cf_ln_b, c_w_out, moe_w_group, moe_b_group, moe_w_router, moe_b_router, moe_w1, moe_w3, moe_w2):
    h = jnp.concatenate([meta_tokens, x[0], jnp.zeros((TP - T, D_MODEL), f32)], axis=0)

    w0 = _prep_in0(a_w_in[0], mla_q_norm[0], mla_w_uq[0], mla_kv_norm[0], mla_w_ukv[0])
    qt, k, vt, nq, nk, nv = _in0(h, norm_mix[0][None], w0)
    mla_t = _mla(qt, k, vt).reshape(HEADS * 64, TP)
    nat = _nat(nq, nk, nv, _nat_bias(nat_rpb[0]))
    nat = nat.transpose(1, 0, 2).reshape(TP, HEADS * 64)
    wo = a_w_out[0].astype(bf16)
    wr, br, w13, w2h, ex = _prep_moe(moe_w_group[0], moe_b_group[0], moe_w_router[0], moe_b_router[0],
                                     moe_w1[0], moe_w3[0], moe_w2[0])
    h, xg, gidx = _out_router(h, mla_t, nat, wo[:512], wo[512:], norm_ffn[0][None], wr, br, a_transposed=True)
    h = _moe(xg, gidx, h, w13, w2h, ex)

    hy_in, cf_in = _in1(h, norm_mix[1][None], c_w_in[0].astype(bf16))
    z = _hyena(hy_in, hy_short_w[0], hy_short_b[0], hy_ffn_w1[0], hy_ffn_b1[0], hy_ffn_w2[0], hy_ffn_b2[0],
               hy_ffn_w3[0], hy_sin_freq[0], hy_log_decay[0], hy_bias[0])
    c = _conf(cf_in, cf_dw_w[0], cf_dw_b[0][None], cf_ln_g[0][None], cf_ln_b[0][None])
    wo = c_w_out[0].astype(bf16)
    wr, br, w13, w2h, ex = _prep_moe(moe_w_group[1], moe_b_group[1], moe_w_router[1], moe_b_router[1],
                                     moe_w1[1], moe_w3[1], moe_w2[1])
    h, xg, gidx = _out_router(h, z, c, wo[:512], wo[512:], norm_ffn[1][None], wr, br)
    h = _moe(xg, gidx, h, w13, w2h, ex)

    return _final(h, norm_final[None])[None]
```

```python
import functools
import math

import numpy as np
import jax
import jax.numpy as jnp
from jax import lax
from jax.experimental import pallas as pl
from jax.experimental.pallas import tpu as pltpu

f32 = jnp.float32
bf16 = jnp.bfloat16

D_MODEL = 1024
SEQ = 16384
N_META = 16
T = SEQ + N_META
TM = 512
TP = 33 * TM
NB = TP // TM
EPS = 1e-6
HEADS = 8
MLA_DK = 96
GRID_W = 64
ROWS = SEQ // GRID_W
WIN_ROWS = 8
WIN_COLS = 16
NEG = -1e30
VT_ROWS = 80
NAT_UNROLL = 16
XG_W = D_MODEL + 128
SORT_TILE = 512
SP = (NB + 4) * SORT_TILE
MLA_UNROLL = 16

HY_W = 512
CF_K = 31
N2 = 128
N1 = 286
NFFT = N1 * N2
K1 = N1 // 2 + 1
N1P = 144
THY = N1P * N2

VMEM_LIMIT = 56 * 1024 * 1024


def _cp(sem, vmem=VMEM_LIMIT):
    return pltpu.CompilerParams(dimension_semantics=sem, vmem_limit_bytes=vmem)


def _rms(x, g):
    return x * lax.rsqrt(jnp.mean(x * x, axis=-1, keepdims=True) + EPS) * g


def _dot(a, b):
    return jnp.dot(a, b, preferred_element_type=f32)


def _dot_hi(a, b):
    return jnp.dot(a, b, preferred_element_type=f32, precision=lax.Precision.HIGHEST)


def _k_in0(h_ref, g_ref, win_ref, qn_ref, kvn_ref, wq_ref, wqr_ref, wk_ref, e_ref, er_ref, wv_ref, cos_ref, sin_ref,
           qt_out, k_out, vt_out, nq_out, nk_out, nv_out):
    i = pl.program_id(0)
    hn = _rms(h_ref[...], g_ref[...]).astype(bf16)
    p = _dot(hn, win_ref[...])
    for hd in range(HEADS):
        nq_out[hd] = p[:, 512 + 64 * hd:576 + 64 * hd].astype(bf16)
        nk_out[hd] = p[:, 1024 + 64 * hd:1088 + 64 * hd].astype(bf16)
        nv_out[hd] = p[:, 1536 + 64 * hd:1600 + 64 * hd].astype(bf16)
    cqn = _rms(p[:, 0:256], qn_ref[...]).astype(bf16)
    ckvn = _rms(p[:, 256:384], kvn_ref[...]).astype(bf16)
    kpe = p[:, 384:512].astype(bf16)
    qa = _dot(cqn, wq_ref[...])
    qr = _dot(cqn, wqr_ref[...])
    ka = _dot(ckvn, wk_ref[...]) + _dot(kpe, e_ref[...])
    kr = _dot(kpe, er_ref[...])
    va = _dot(ckvn, wv_ref[...])
    cos = cos_ref[...]
    sin = sin_ref[...]
    t = i * TM + lax.broadcasted_iota(jnp.int32, (TM, 1), 0)
    lane = lax.broadcasted_iota(jnp.int32, (1, 128), 1)
    ones_col = jnp.where((t < T) & (lane == 64), 1.0, 0.0)
    for hd in range(HEADS):
        sl = slice(hd * 128, (hd + 1) * 128)
        qt_out[sl, :] = (qa[:, sl] * cos + qr[:, sl] * sin).T.astype(bf16)
        k_out[hd] = (ka[:, sl] * cos + kr[:, sl] * sin).astype(bf16)
        vt_out[hd, 0] = (va[:, sl] + ones_col).T[:VT_ROWS].astype(bf16)


def _in0(h, g, w):
    full = lambda shape: pl.BlockSpec(shape, lambda i: (0,) * len(shape))
    row = lambda n: pl.BlockSpec((TM, n), lambda i: (i, 0))
    hm = lambda n: pl.BlockSpec((HEADS, TM, n), lambda i: (0, i, 0))
    outs = [jax.ShapeDtypeStruct((HEADS * 128, TP), bf16), jax.ShapeDtypeStruct((HEADS, TP, 128), bf16),
            jax.ShapeDtypeStruct((HEADS, NB, VT_ROWS, TM), bf16)] + [jax.ShapeDtypeStruct((HEADS, TP, 64), bf16)] * 3
    return pl.pallas_call(
        _k_in0,
        grid=(NB,),
        in_specs=[row(1024), full((1, 1024)), full((1024, 2048)), full((1, 256)), full((1, 128)),
                  full((256, 1024)), full((256, 1024)), full((128, 1024)), full((128, 1024)), full((128, 1024)),
                  full((128, 1024)), row(128), row(128)],
        out_specs=[pl.BlockSpec((HEADS * 128, TM), lambda i: (0, i)), hm(128),
                   pl.BlockSpec((HEADS, 1, VT_ROWS, TM), lambda i: (0, i, 0, 0)), hm(64), hm(64), hm(64)],
        out_shape=outs,
        compiler_params=_cp(("parallel",)),
        name="in0",
    )(h, g, w["win"], w["qn"], w["kvn"], w["wq"], w["wqr"], w["wk"], w["e"], w["er"], w["wv"], w["cos"], w["sin"])


def _k_mla(qt_ref, k_ref, vt_ref, o_ref, sa_ref, sb_ref, m_ref, acc_ref):
    qt = qt_ref[...]
    m_ref[...] = jnp.full((1, TM), NEG, f32)
    acc_ref[...] = jnp.zeros((VT_ROWS, TM), f32)

    def scores(j):
        return _dot(k_ref[0, j], qt)

    def update(s_ref, j, masked):
        s = s_ref[...]
        if masked:
            key = lax.broadcasted_iota(jnp.int32, (TM, 1), 0)
            s = jnp.where(key < T - (NB - 1) * TM, s, NEG)
        m = m_ref[...]
        m_new = jnp.maximum(m, s.max(axis=0, keepdims=True))
        p = jnp.exp2(s - m_new).astype(bf16)
        acc_ref[...] = jnp.exp2(m - m_new) * acc_ref[...] + _dot(vt_ref[0, j], p)
        m_ref[...] = m_new

    sa_ref[...] = scores(0)

    def group(i, _):
        for u in range(0, MLA_UNROLL, 2):
            j = MLA_UNROLL * i + u
            sb_ref[...] = scores(j + 1)
            update(sa_ref, j, False)
            sa_ref[...] = scores(j + 2)
            update(sb_ref, j + 1, False)
        return 0

    lax.fori_loop(0, (NB - 1) // MLA_UNROLL, group, 0)
    update(sa_ref, NB - 1, True)
    acc = acc_ref[...]
    o_ref[0] = (acc[:64] / acc[64:65]).astype(bf16)


def _mla(qt, k, vt):
    return pl.pallas_call(
        _k_mla,
        grid=(HEADS, NB),
        in_specs=[pl.BlockSpec((128, TM), lambda h, i: (h, i)),
                  pl.BlockSpec((1, NB, TM, 128), lambda h, i: (h, 0, 0, 0)),
                  pl.BlockSpec((1, NB, VT_ROWS, TM), lambda h, i: (h, 0, 0, 0))],
        out_specs=pl.BlockSpec((1, 64, TM), lambda h, i: (h, 0, i)),
        out_shape=jax.ShapeDtypeStruct((HEADS, 64, TP), bf16),
        scratch_shapes=[pltpu.VMEM((TM, TM), f32), pltpu.VMEM((TM, TM), f32), pltpu.VMEM((1, TM), f32),
                        pltpu.VMEM((VT_ROWS, TM), f32)],
        compiler_params=_cp(("parallel", "parallel")),
        name="mla",
    )(qt, k.reshape(HEADS, NB, TM, 128), vt)


def _k_nat(q_ref, k_ref, v_ref, c_ref, o_ref):
    km = k_ref[0, 0:N_META, :]
    vm = v_ref[0, 0:N_META, :]
    qm = q_ref[0, 0:N_META, :]
    s = lax.dot_general(qm, km, (((1,), (1,)), ((), ())), preferred_element_type=f32)
    p = jnp.exp(s - s.max(axis=1, keepdims=True))
    o_ref[0, 0:N_META, :] = (_dot(p.astype(bf16), vm) / p.sum(axis=1, keepdims=True)).astype(bf16)
    o_ref[0, T:TP, :] = jnp.zeros((TP - T, 64), bf16)

    def rows(ii, _):
        idx = [ii * NAT_UNROLL + r for r in range(NAT_UNROLL)]
        r0s = [jnp.clip(i - WIN_ROWS // 2, 0, ROWS - WIN_ROWS) for i in idx]
        qoffs = [pl.multiple_of(N_META + GRID_W * i, 16) for i in idx]
        koffs = [pl.multiple_of(N_META + GRID_W * r0, 16) for r0 in r0s]
        ss, sms = [], []
        for i, r0, qoff, koff in zip(idx, r0s, qoffs, koffs):
            a0 = r0 - i + WIN_ROWS - 1
            qi = q_ref[0, pl.ds(qoff, GRID_W), :]
            kw = k_ref[0, pl.ds(koff, WIN_ROWS * GRID_W), :]
            s = lax.dot_general(qi, kw, (((1,), (1,)), ((), ())), preferred_element_type=f32)
            ss.append(s + jnp.concatenate([c_ref[0, a0 + 2 * j] for j in range(WIN_ROWS // 2)], axis=1))
            sms.append(lax.dot_general(qi, km, (((1,), (1,)), ((), ())), preferred_element_type=f32))
        ps, pms, ls = [], [], []
        for s, sm in zip(ss, sms):
            m = jnp.maximum(s.max(axis=1, keepdims=True), sm.max(axis=1, keepdims=True))
            p = jnp.exp(s - m)
            pm = jnp.exp(sm - m)
            ls.append(p.sum(axis=1, keepdims=True) + pm.sum(axis=1, keepdims=True))
            ps.append(p.astype(bf16))
            pms.append(pm.astype(bf16))
        for p, pm, l, qoff, koff in zip(ps, pms, ls, qoffs, koffs):
            vw = v_ref[0, pl.ds(koff, WIN_ROWS * GRID_W), :]
            o = _dot(p, vw) + _dot(pm, vm)
            o_ref[0, pl.ds(qoff, GRID_W), :] = (o / l).astype(bf16)
        return 0

    lax.fori_loop(0, ROWS // NAT_UNROLL, rows, 0)


def _nat(q, k, v, c2):
    blk = pl.BlockSpec((1, TP, 64), lambda h: (h, 0, 0))
    return pl.pallas_call(
        _k_nat,
        grid=(HEADS,),
        in_specs=[blk, blk, blk, pl.BlockSpec((1, 2 * WIN_ROWS - 2, GRID_W, 2 * GRID_W), lambda h: (h, 0, 0, 0))],
        out_specs=blk,
        out_shape=jax.ShapeDtypeStruct((HEADS, TP, 64), bf16),
        compiler_params=_cp(("parallel",)),
        name="nat",
    )(q, k, v, c2)


def _k_out_router(h_ref, a_ref, b_ref, wa_ref, wb_ref, g_ref, wr_ref, br_ref, h_out, xg_out, gidx_out, *, a_transposed):
    if a_transposed:
        ya = lax.dot_general(a_ref[...], wa_ref[...], (((0,), (0,)), ((), ())), preferred_element_type=f32)
    else:
        ya = _dot(a_ref[...].reshape(TM, 512).astype(bf16), wa_ref[...])
    h1 = h_ref[...] + ya + _dot(b_ref[...].astype(bf16), wb_ref[...])
    h_out[...] = h1
    xn = _rms(h1, g_ref[...])
    xg_out[:, :D_MODEL] = xn
    xh = xn.astype(bf16)
    xl = (xn - xh.astype(f32)).astype(bf16)
    wr = wr_ref[...]
    wh = wr.astype(bf16)
    wl = (wr - wh.astype(f32)).astype(bf16)
    logits = _dot(xh, wh) + _dot(xh, wl) + _dot(xl, wh) + br_ref[...]
    lane = lax.broadcasted_iota(jnp.int32, logits.shape, 1)
    big = jnp.int32(1 << 20)
    gl = jnp.where((lane >= 32) & (lane < 36), logits, NEG)
    gmax = gl.max(axis=1, keepdims=True)
    gsum = jnp.exp(gl - gmax).sum(axis=1, keepdims=True)
    p_group = 1.0 / gsum
    gidx = jnp.clip(jnp.where(gl == gmax, lane, big).min(axis=1, keepdims=True) - 32, 0, 3)
    el = jnp.where((lane >> 3) == gidx, logits, NEG)
    v1 = el.max(axis=1, keepdims=True)
    i1 = jnp.where(el == v1, lane, big).min(axis=1, keepdims=True)
    el2 = jnp.where(lane == i1, NEG, el)
    v2 = el2.max(axis=1, keepdims=True)
    i2 = jnp.where(el2 == v2, lane, big).min(axis=1, keepdims=True)
    e2 = jnp.exp(v2 - v1)
    w1 = p_group / (1.0 + e2)
    w2 = w1 * e2
    xg_out[:, D_MODEL:] = jnp.where(lane == i1, w1, jnp.where(lane == i2, w2, 0.0))
    gidx_out[...] = gidx


def _out_router(h, a, b, wa, wb, g, wr, br, a_transposed=False):
    full = lambda shape: pl.BlockSpec(shape, lambda i: (0,) * len(shape))
    row = lambda n: pl.BlockSpec((TM, n), lambda i: (i, 0))
    a_spec = (pl.BlockSpec((512, TM), lambda i: (0, i)) if a_transposed
              else pl.BlockSpec((TM // N2, N2, 512), lambda i: (i, 0, 0)))
    return pl.pallas_call(
        functools.partial(_k_out_router, a_transposed=a_transposed),
        grid=(NB,),
        in_specs=[row(1024), a_spec, row(512), full((512, 1024)), full((512, 1024)), full((1, 1024)),
                  full((1024, 128)), full((1, 128))],
        out_specs=[row(1024), row(XG_W), row(1)],
        out_shape=[jax.ShapeDtypeStruct((TP, 1024), f32), jax.ShapeDtypeStruct((TP, XG_W), f32),
                   jax.ShapeDtypeStruct((TP, 1), jnp.int32)],
        compiler_params=_cp(("parallel",)),
        name="out_router",
    )(h, a, b, wa, wb, g, wr, br)


def _k_rank(g_ref, rank_out, cnt_out, carry_ref):
    @pl.when(pl.program_id(0) == 0)
    def _():
        carry_ref[...] = jnp.zeros((1, 128), f32)

    lane = lax.broadcasted_iota(jnp.int32, (TM, 128), 1)
    oh = jnp.where(lane == g_ref[...], 1.0, 0.0)
    r = lax.broadcasted_iota(jnp.int32, (TM, TM), 0)
    c = lax.broadcasted_iota(jnp.int32, (TM, TM), 1)
    earlier = jnp.where(c < r, 1.0, 0.0).astype(bf16)
    excl = _dot(earlier, oh.astype(bf16)) + carry_ref[...]
    rank_out[...] = (oh * excl).sum(axis=1, keepdims=True).astype(jnp.int32)
    carry_ref[...] += oh.sum(axis=0, keepdims=True)
    cnt_out[...] = carry_ref[...]


def _rank(gidx):
    return pl.pallas_call(
        _k_rank,
        grid=(NB,),
        in_specs=[pl.BlockSpec((TM, 1), lambda i: (i, 0))],
        out_specs=[pl.BlockSpec((TM, 1), lambda i: (i, 0)), pl.BlockSpec((1, 128), lambda i: (0, 0))],
        out_shape=[jax.ShapeDtypeStruct((TP, 1), jnp.int32), jax.ShapeDtypeStruct((1, 128), f32)],
        scratch_shapes=[pltpu.VMEM((1, 128), f32)],
        compiler_params=_cp(("arbitrary",)),
        name="moe_rank",
    )(gidx)


def _row_copy(src_ref, src_row, dst_ref, dst_row, sem):
    return pltpu.make_async_copy(src_ref.at[pl.ds(src_row, 1)], dst_ref.at[pl.ds(dst_row, 1)], sem)


def _k_dispatch(pos_ref, x_ref, xs_in_ref, xs_ref, sem):
    del xs_in_ref
    base = pl.program_id(0) * TM

    def start(r, _):
        _row_copy(x_ref, r, xs_ref, pos_ref[base + r], sem).start()
        return 0

    lax.fori_loop(0, TM, start, 0, unroll=8)
    pltpu.make_async_copy(x_ref, xs_ref.at[pl.ds(0, TM)], sem).wait()


def _dispatch(pos, xg):
    return pl.pallas_call(
        _k_dispatch,
        grid_spec=pltpu.PrefetchScalarGridSpec(
            num_scalar_prefetch=1, grid=(NB,),
            in_specs=[pl.BlockSpec((TM, XG_W), lambda i, pos: (i, 0)), pl.BlockSpec(memory_space=pl.ANY)],
            out_specs=pl.BlockSpec(memory_space=pl.ANY),
            scratch_shapes=[pltpu.SemaphoreType.DMA(())]),
        out_shape=jax.ShapeDtypeStruct((SP, XG_W), f32),
        input_output_aliases={2: 0},
        compiler_params=_cp(("arbitrary",)),
        name="moe_dispatch",
    )(pos, xg, jnp.zeros((SP, XG_W), f32))


def _k_moe(tg_ref, nt_ref, xs_ref, w13_ref, w2_ref, ex_ref, ys_ref):
    del tg_ref
    live = pl.program_id(0) < nt_ref[0]
    half = pl.program_id(1)

    @pl.when(jnp.logical_not(live) & (half == 0))
    def _():
        ys_ref[...] = jnp.zeros((SORT_TILE, D_MODEL), f32)

    @pl.when(live)
    def _():
        x = xs_ref[...]
        a = _dot(x[:, :D_MODEL].astype(bf16), w13_ref[0, 0])
        gate = x[:, D_MODEL:]
        ghi = gate.astype(bf16)
        glo = (gate - ghi.astype(f32)).astype(bf16)
        gx = _dot(ghi, ex_ref[0, 0]) + _dot(glo, ex_ref[0, 0])
        a1 = a[:, :1024]
        hid = a1 * jax.nn.sigmoid(a1) * a[:, 1024:] * gx
        y = _dot(hid.astype(bf16), w2_ref[0, 0])

        @pl.when(half == 0)
        def _():
            ys_ref[...] = y

        @pl.when(half == 1)
        def _():
            ys_ref[...] += y


def _moe_sorted(tile_group, n_tiles, xs, w13, w2, ex):
    wspec = lambda shape: pl.BlockSpec((1, 1) + shape, lambda j, s, tg, nt: (tg[j], s, 0, 0))
    return pl.pallas_call(
        _k_moe,
        grid_spec=pltpu.PrefetchScalarGridSpec(
            num_scalar_prefetch=2, grid=(SP // SORT_TILE, 2),
            in_specs=[pl.BlockSpec((SORT_TILE, XG_W), lambda j, s, tg, nt: (j, 0)),
                      wspec((1024, 2048)), wspec((1024, 1024)), wspec((128, 1024))],
            out_specs=pl.BlockSpec((SORT_TILE, D_MODEL), lambda j, s, tg, nt: (j, 0))),
        out_shape=jax.ShapeDtypeStruct((SP, D_MODEL), f32),
        compiler_params=_cp(("arbitrary", "arbitrary")),
        name="moe",
    )(tile_group, n_tiles, xs, w13, w2, ex)


def _k_combine(pos_ref, h_ref, ys_ref, o_ref, buf_ref, sem):
    base = pl.program_id(0) * TM

    def start(r, _):
        _row_copy(ys_ref, pos_ref[base + r], buf_ref, r, sem).start()
        return 0

    lax.fori_loop(0, TM, start, 0, unroll=8)
    pltpu.make_async_copy(ys_ref.at[pl.ds(0, TM)], buf_ref, sem).wait()
    o_ref[...] = h_ref[...] + buf_ref[...]


def _combine(pos, h, ys):
    return pl.pallas_call(
        _k_combine,
        grid_spec=pltpu.PrefetchScalarGridSpec(
            num_scalar_prefetch=1, grid=(NB,),
            in_specs=[pl.BlockSpec((TM, D_MODEL), lambda i, pos: (i, 0)), pl.BlockSpec(memory_space=pl.ANY)],
            out_specs=pl.BlockSpec((TM, D_MODEL), lambda i, pos: (i, 0)),
            scratch_shapes=[pltpu.VMEM((TM, D_MODEL), f32), pltpu.SemaphoreType.DMA(())]),
        out_shape=jax.ShapeDtypeStruct((TP, D_MODEL), f32),
        compiler_params=_cp(("arbitrary",)),
        name="moe_combine",
    )(pos, h, ys)


def _moe(xg, gidx, h, w13, w2, ex):
    rank, cnt = _rank(gidx)
    cnt = cnt[0, :4].astype(jnp.int32)
    tiles = (cnt + SORT_TILE - 1) // SORT_TILE
    tile_end = jnp.cumsum(tiles)
    start = (tile_end - tiles) * SORT_TILE
    pos = start[gidx[:, 0]] + rank[:, 0]
    tile_group = jnp.minimum(jnp.sum(jnp.arange(SP // SORT_TILE)[:, None] >= tile_end[None, :], axis=1), 3)
    xs = _dispatch(pos, xg)
    ys = _moe_sorted(tile_group.astype(jnp.int32), tile_end[3:4].astype(jnp.int32), xs, w13, w2, ex)
    return _combine(pos, h, ys)


def _k_in1(h_ref, g_ref, w_ref, hy_out, cf_out):
    hn = _rms(h_ref[...], g_ref[...]).astype(bf16)
    p = _dot(hn, w_ref[...])
    hy_out[...] = p[:, :1536]
    cf_out[...] = p[:, 1536:]


def _in1(h, g, w):
    row = lambda n: pl.BlockSpec((TM, n), lambda i: (i, 0))
    return pl.pallas_call(
        _k_in1,
        grid=(NB,),
        in_specs=[row(1024), pl.BlockSpec((1, 1024), lambda i: (0, 0)), pl.BlockSpec((1024, 2560), lambda i: (0, 0))],
        out_specs=[row(1536), row(1024)],
        out_shape=[jax.ShapeDtypeStruct((TP, 1536), f32), jax.ShapeDtypeStruct((TP, 1024), f32)],
        compiler_params=_cp(("parallel",)),
        name="in1",
    )(h, g, w)


def _k_short(prev_ref, cur_ref, next_ref, w_ref, b_ref, v_out, x1_out, x2_out):
    i = pl.program_id(0)
    ext = jnp.concatenate([prev_ref[...], cur_ref[...], next_ref[...]], axis=0)
    t = i * TM - 8 + lax.broadcasted_iota(jnp.int32, (TM + 16, 1), 0)
    ext = jnp.where((t >= 0) & (t < T), ext, 0.0)
    w = w_ref[...]
    s = ext[7:7 + TM] * w[0:1] + ext[8:8 + TM] * w[1:2] + ext[9:9 + TM] * w[2:3] + b_ref[...]
    s = jnp.where(t[8:8 + TM] < T, s, 0.0)
    v_out[...] = s[:, :512].reshape(TM // N2, N2, 512)
    x1_out[...] = s[:, 512:1024].reshape(TM // N2, N2, 512)
    x2_out[...] = s[:, 1024:].reshape(TM // N2, N2, 512)


def _short(hy, w, b):
    last = NB - 1
    nblk = THY // TM
    return pl.pallas_call(
        _k_short,
        grid=(nblk,),
        in_specs=[pl.BlockSpec((8, 1536), lambda i: (jnp.maximum(jnp.minimum(i, last) * (TM // 8) - 1, 0), 0)),
                  pl.BlockSpec((TM, 1536), lambda i: (jnp.minimum(i, last), 0)),
                  pl.BlockSpec((8, 1536), lambda i: (jnp.minimum((jnp.minimum(i, last) + 1) * (TM // 8), TP // 8 - 1), 0)),
                  pl.BlockSpec((3, 1536), lambda i: (0, 0)), pl.BlockSpec((1, 1536), lambda i: (0, 0))],
        out_specs=[pl.BlockSpec((TM // N2, N2, 512), lambda i: (i, 0, 0))] * 3,
        out_shape=[jax.ShapeDtypeStruct((N1P, N2, 512), f32)] * 3,
        compiler_params=_cp(("parallel",)),
        name="hy_short",
    )(hy, hy, hy, w, b)


def _k_filt(feat_ref, w1_ref, b1_ref, w2_ref, b2_ref, w3_ref, fr_ref, rate_ref, o_ref):
    i = pl.program_id(0)
    feats = feat_ref[...]
    fr = fr_ref[...]
    hid = jnp.sin(fr * (_dot_hi(feats, w1_ref[...]) + b1_ref[...]))
    hid = jnp.sin(fr * (_dot_hi(hid, w2_ref[...]) + b2_ref[...]))
    filt = _dot(hid.astype(bf16), w3_ref[...].astype(bf16))
    filt = filt * jnp.exp(-feats[:, 0:1] * rate_ref[...])
    lag = i * TM + lax.broadcasted_iota(jnp.int32, (TM, 1), 0)
    col = lax.broadcasted_iota(jnp.int32, (1, 2048), 1)
    keep = (lag < T) & ((lag > 0) | (col < 1024))
    o_ref[...] = jnp.where(keep, filt, 0.0).reshape(TM // N2, N2, 2048)


def _filt(feats, w1, b1, w2, b2, w3, fr, rate):
    full = lambda shape: pl.BlockSpec(shape, lambda i: (0,) * len(shape))
    return pl.pallas_call(
        _k_filt,
        grid=(THY // TM,),
        in_specs=[pl.BlockSpec((TM, 128), lambda i: (i, 0)), full((128, 64)), full((1, 64)), full((64, 64)),
                  full((1, 64)), full((64, 2048)), full((1, 64)), full((1, 2048))],
        out_specs=pl.BlockSpec((TM // N2, N2, 2048), lambda i: (i, 0, 0)),
        out_shape=jax.ShapeDtypeStruct((N1P, N2, 2048), f32),
        compiler_params=_cp(("parallel",)),
        name="hy_filt",
    )(feats, w1, b1, w2, b2, w3, fr, rate)


def _k_dft1(f_ref, x_ref, y_ref):
    f = f_ref[...]
    for j in range(8):
        y = _dot(f, x_ref[:, j, :].astype(bf16))
        y_ref[:, 0, j, :] = y[:K1]
        y_ref[:, 1, j, :] = y[K1:]


def _dft1(f1c, x3):
    c = x3.shape[2]
    return pl.pallas_call(
        _k_dft1,
        grid=(N2 // 8, c // 512),
        in_specs=[pl.BlockSpec((2 * K1, N1P), lambda i, cb: (0, 0)),
                  pl.BlockSpec((N1P, 8, 512), lambda i, cb: (0, i, cb))],
        out_specs=pl.BlockSpec((K1, 2, 8, 512), lambda i, cb: (0, 0, i, cb)),
        out_shape=jax.ShapeDtypeStruct((K1, 2, N2, c), f32),
        compiler_params=_cp(("parallel", "parallel")),
        name="dft1",
    )(f1c, x3)


def _stage2_matrix(f2_ref, tw_ref):
    fr, fi = f2_ref[0], f2_ref[1]
    tr, ti = tw_ref[0, 0:1, :], tw_ref[0, 1:2, :]
    gre = fr * tr - fi * ti
    gim = fr * ti + fi * tr
    return jnp.concatenate([jnp.concatenate([gre, -gim], axis=1), jnp.concatenate([gim, gre], axis=1)], axis=0)


def _k_spec_filt(y_ref, f2_ref, tw_ref, o_ref):
    y = y_ref[0].reshape(2 * N2, 2048).astype(bf16)
    x = _dot(_stage2_matrix(f2_ref, tw_ref).astype(bf16), y)
    o_ref[0, 0] = x[:N2, :1024] + x[:N2, 1024:]
    o_ref[0, 1] = x[N2:, :1024] - x[N2:, 1024:]


def _spec_filt(yf, f2, tw):
    return pl.pallas_call(
        _k_spec_filt,
        grid=(K1,),
        in_specs=[pl.BlockSpec((1, 2, N2, 2048), lambda k: (k, 0, 0, 0)),
                  pl.BlockSpec((2, N2, N2), lambda k: (0, 0, 0)), pl.BlockSpec((1, 2, N2), lambda k: (k, 0, 0))],
        out_specs=pl.BlockSpec((1, 2, N2, 1024), lambda k: (k, 0, 0, 0)),
        out_shape=jax.ShapeDtypeStruct((K1, 2, N2, 1024), f32),
        compiler_params=_cp(("parallel",)),
        name="spec_filt",
    )(yf, f2, tw)


def _k_spec_conv(y_ref, kf_ref, f2_ref, tw_ref, z_ref):
    g = _stage2_matrix(f2_ref, tw_ref)
    y = y_ref[0].reshape(2 * N2, 512).astype(bf16)
    x = _dot(g.astype(bf16), y)
    xr, xi = x[:N2], x[N2:]
    kr, ki = kf_ref[0, 0], kf_ref[0, 1]
    p = jnp.concatenate([xr * kr - xi * ki, xr * ki + xi * kr], axis=0).astype(bf16)
    z_ref[0] = _dot(g.T.astype(bf16), p).reshape(2, N2, 512)


def _spec_conv(y, kf, order, f2, tw):
    return pl.pallas_call(
        _k_spec_conv,
        grid=(K1,),
        in_specs=[pl.BlockSpec((1, 2, N2, 512), lambda k: (k, 0, 0, 0)),
                  pl.BlockSpec((1, 2, N2, 512), lambda k: (k, 0, 0, order)),
                  pl.BlockSpec((2, N2, N2), lambda k: (0, 0, 0)), pl.BlockSpec((1, 2, N2), lambda k: (k, 0, 0))],
        out_specs=pl.BlockSpec((1, 2, N2, 512), lambda k: (k, 0, 0, 0)),
        out_shape=jax.ShapeDtypeStruct((K1, 2, N2, 512), f32),
        compiler_params=_cp(("parallel",)),
        name="spec_conv",
    )(y, kf, f2, tw)


def _k_idft(f_ref, z_ref, zin_ref, gate_ref, bias_ref, o_ref):
    f = f_ref[...]
    bias = bias_ref[...]
    for j in range(8):
        zj = jnp.concatenate([z_ref[:, 0, j, :], z_ref[:, 1, j, :]], axis=0).astype(bf16)
        y = _dot(f, zj)
        o_ref[:, j, :] = gate_ref[:, j, :] * (y + zin_ref[:, j, :] * bias)


def _idft(f1inv, z, zin, gate, bias):
    blk = pl.BlockSpec((N1P, 8, 512), lambda i: (0, i, 0))
    return pl.pallas_call(
        _k_idft,
        grid=(N2 // 8,),
        in_specs=[pl.BlockSpec((N1P, 2 * K1), lambda i: (0, 0)),
                  pl.BlockSpec((K1, 2, 8, 512), lambda i: (0, 0, i, 0)), blk, blk,
                  pl.BlockSpec((1, 512), lambda i: (0, 0))],
        out_specs=blk,
        out_shape=jax.ShapeDtypeStruct((N1P, N2, 512), f32),
        compiler_params=_cp(("parallel",)),
        name="idft",
    )(f1inv, z, zin, gate, bias)


def _k_conf(prev_ref, cur_ref, next_ref, w_ref, b_ref, lg_ref, lb_ref, o_ref, sh_ref):
    i = pl.program_id(0)
    ext = jnp.concatenate([prev_ref[...], cur_ref[...], next_ref[...]], axis=0)
    t = i * TM - 16 + lax.broadcasted_iota(jnp.int32, (TM + 32, 1), 0)
    u = ext[:, :512] * jax.nn.sigmoid(ext[:, 512:])
    u = jnp.where((t >= 0) & (t < T), u, 0.0)
    w = w_ref[...]
    acc = jnp.zeros((TM, 512), f32) + b_ref[...]
    for b in range(8):
        sh_ref[b] = u[b:b + TM + 24]
    for j in range(CF_K):
        a, b = divmod(j + 1, 8)
        acc = acc + sh_ref[b, 8 * a:8 * a + TM, :] * w[j:j + 1]
    mu = jnp.mean(acc, axis=-1, keepdims=True)
    xc = acc - mu
    y = xc * lax.rsqrt(jnp.mean(xc * xc, axis=-1, keepdims=True) + EPS) * lg_ref[...] + lb_ref[...]
    o_ref[...] = (y * jax.nn.sigmoid(y)).astype(bf16)


def _conf(cf, w, b, lg, lb):
    full = lambda shape: pl.BlockSpec(shape, lambda i: (0,) * len(shape))
    return pl.pallas_call(
        _k_conf,
        grid=(NB,),
        in_specs=[pl.BlockSpec((16, 1024), lambda i: (jnp.maximum(i * (TM // 16) - 1, 0), 0)),
                  pl.BlockSpec((TM, 1024), lambda i: (i, 0)),
                  pl.BlockSpec((16, 1024), lambda i: (jnp.minimum((i + 1) * (TM // 16), TP // 16 - 1), 0)),
                  full((CF_K, 512)), full((1, 512)), full((1, 512)), full((1, 512))],
        out_specs=pl.BlockSpec((TM, 512), lambda i: (i, 0)),
        out_shape=jax.ShapeDtypeStruct((TP, 512), bf16),
        scratch_shapes=[pltpu.VMEM((8, TM + 24, 512), f32)],
        compiler_params=_cp(("parallel",)),
        name="conformer",
    )(cf, cf, cf, w, b, lg, lb)


def _k_final(h_ref, nxt_ref, g_ref, o_ref):
    h = jnp.concatenate([h_ref[N_META:, :], nxt_ref[...]], axis=0)
    o_ref[...] = _rms(h, g_ref[...])


def _final(h, g):
    return pl.pallas_call(
        _k_final,
        grid=(SEQ // TM,),
        in_specs=[pl.BlockSpec((TM, 1024), lambda i: (i, 0)),
                  pl.BlockSpec((N_META, 1024), lambda i: ((i + 1) * (TM // N_META), 0)),
                  pl.BlockSpec((1, 1024), lambda i: (0, 0))],
        out_specs=pl.BlockSpec((TM, 1024), lambda i: (i, 0)),
        out_shape=jax.ShapeDtypeStruct((SEQ, 1024), f32),
        compiler_params=_cp(("parallel",)),
        name="final_norm",
    )(h, h, g)


def _prep_in0(a_w_in, q_norm, w_uq, kv_norm, w_ukv):
    win = jnp.concatenate([a_w_in[:, :416], jnp.zeros((D_MODEL, 96), f32), a_w_in[:, 416:928] * (64 ** -0.5),
                           a_w_in[:, 928:]], axis=1).astype(bf16)
    wq3 = (w_uq * (MLA_DK ** -0.5 * math.log2(math.e))).reshape(256, HEADS, MLA_DK)
    rope = wq3[..., 64:]
    rot = jnp.concatenate([-rope[..., 16:], rope[..., :16]], axis=-1)
    z64 = jnp.zeros((256, HEADS, 64), f32)
    z32 = jnp.zeros((256, HEADS, 32), f32)
    wq = jnp.concatenate([wq3, z32], axis=-1).reshape(256, 1024).astype(bf16)
    wqr = jnp.concatenate([z64, rot, z32], axis=-1).reshape(256, 1024).astype(bf16)
    kv3 = w_ukv.reshape(128, HEADS, 128)
    wk = jnp.concatenate([kv3[..., :64], jnp.zeros((128, HEADS, 64), f32)], axis=-1).reshape(128, 1024).astype(bf16)
    wv = jnp.concatenate([kv3[..., 64:], jnp.zeros((128, HEADS, 64), f32)], axis=-1).reshape(128, 1024).astype(bf16)
    e = np.zeros((128, HEADS, 128), np.float32)
    er = np.zeros((128, HEADS, 128), np.float32)
    for r in range(32):
        e[r, :, 64 + r] = 1.0
        if r < 16:
            er[r + 16, :, 64 + r] = -1.0
        else:
            er[r - 16, :, 64 + r] = 1.0
    pos = jnp.arange(TP, dtype=f32)
    inv_freq = 10000.0 ** (-jnp.arange(0, 32, 2, dtype=f32) / 32)
    ang = pos[:, None] * inv_freq[None, :]
    c, s = jnp.cos(ang), jnp.sin(ang)
    cos = jnp.concatenate([jnp.ones((TP, 64), f32), c, c, jnp.zeros((TP, 32), f32)], axis=1)
    sin = jnp.concatenate([jnp.zeros((TP, 64), f32), s, s, jnp.zeros((TP, 32), f32)], axis=1)
    return dict(win=win, qn=q_norm[None], kvn=kv_norm[None], wq=wq, wqr=wqr, wk=wk,
                e=jnp.asarray(e.reshape(128, 1024)).astype(bf16), er=jnp.asarray(er.reshape(128, 1024)).astype(bf16),
                wv=wv, cos=cos, sin=sin)


def _nat_bias(rpb):
    qc = np.arange(GRID_W)[:, None]
    kc = np.arange(GRID_W)[None, :]
    c0 = np.clip(qc - WIN_COLS // 2, 0, GRID_W - WIN_COLS)
    inside = (kc >= c0) & (kc < c0 + WIN_COLS)
    idx = np.clip(kc - qc + WIN_COLS - 1, 0, 2 * WIN_COLS - 2)
    c = jnp.where(jnp.asarray(inside)[None, None], rpb[:, :, idx], NEG)
    return jnp.concatenate([c[:, :-1], c[:, 1:]], axis=-1)


def _prep_moe(w_group, b_group, w_router, b_router, w1, w3, w2):
    wr = jnp.concatenate([w_router.transpose(1, 0, 2).reshape(D_MODEL, 32), w_group,
                          jnp.zeros((D_MODEL, 92), f32)], axis=1)
    br = jnp.concatenate([b_router.reshape(32), b_group, jnp.zeros((92,), f32)])[None]

    def half(w):
        return w.reshape(4, 2, 4, D_MODEL, 256).transpose(0, 1, 3, 2, 4).reshape(4, 2, D_MODEL, 1024)

    w13 = jnp.concatenate([half(w1), half(w3)], axis=-1).astype(bf16)
    w2h = w2.reshape(4, 2, 1024, D_MODEL).astype(bf16)
    ex = np.zeros((4, 2, 128, 4, 256), np.float32)
    for g in range(4):
        for s in range(2):
            for e in range(4):
                ex[g, s, g * 8 + s * 4 + e, e, :] = 1.0
    return wr, br, w13, w2h, jnp.asarray(ex.reshape(4, 2, 128, 1024)).astype(bf16)


def _dft_tables():
    two_pi = 2.0 * math.pi
    k1 = jnp.arange(K1, dtype=jnp.int32)
    n1 = jnp.arange(N1P, dtype=jnp.int32)
    th = two_pi * ((k1[:, None] * n1[None, :]) % N1).astype(f32) / N1
    f1c = jnp.concatenate([jnp.cos(th), -jnp.sin(th)], axis=0).astype(bf16)
    wgt = jnp.where((k1 == 0) | (k1 == N1 // 2), 1.0, 2.0).astype(f32) / NFFT
    f1inv = jnp.concatenate([jnp.cos(th).T * wgt[None, :], -jnp.sin(th).T * wgt[None, :]], axis=1).astype(bf16)
    n2 = jnp.arange(N2, dtype=jnp.int32)
    a2 = -two_pi * ((n2[:, None] * n2[None, :]) % N2).astype(f32) / N2
    f2 = jnp.stack([jnp.cos(a2), jnp.sin(a2)])
    at = -two_pi * ((k1[:, None] * n2[None, :]) % NFFT).astype(f32) / NFFT
    tw = jnp.stack([jnp.cos(at), jnp.sin(at)], axis=1)
    return f1c, f1inv, f2, tw


def _hyena(hy_in, short_w, short_b, f_w1, f_b1, f_w2, f_b2, f_w3, sin_freq, log_decay, hy_bias):
    v, x1, x2 = _short(hy_in, short_w, short_b[None])
    f1c, f1inv, f2, tw = _dft_tables()
    tt = jnp.linspace(0.0, 1.0, T, dtype=f32)
    bands = jnp.linspace(1e-4, 15.0, 16, dtype=f32)
    ang = 2.0 * math.pi * tt[:, None] * bands[None, :]
    feats = jnp.concatenate([tt[:, None], jnp.cos(ang), jnp.sin(ang), jnp.zeros((T, 95), f32)], axis=-1)
    feats = jnp.concatenate([feats, jnp.zeros((THY - T, 128), f32)], axis=0)
    w1p = jnp.concatenate([f_w1, jnp.zeros((95, 64), f32)], axis=0)
    rate = jnp.exp(log_decay.astype(f32)).reshape(1, 2048)
    filt = _filt(feats, w1p, f_b1[None], f_w2, f_b2[None], f_w3, sin_freq[None], rate)
    kf = _spec_filt(_dft1(f1c, filt), f2, tw)
    z = v
    for o, gate in enumerate((x1, x2)):
        zf = _spec_conv(_dft1(f1c, z), kf, o, f2, tw)
        z = _idft(f1inv, zf, z, gate, hy_bias[o][None])
    return z


def kernel(x, meta_tokens, norm_mix, norm_ffn, norm_final, a_w_in, mla_q_norm, mla_w_uq, mla_kv_norm, mla_w_ukv, nat_rpb, a_w_out, c_w_in, hy_short_w, hy_short_b, hy_ffn_w1, hy_ffn_b1, hy_ffn_w2, hy_ffn_b2, hy_ffn_w3, hy_sin_freq, hy_log_decay, hy_bias, cf_dw_w, cf_dw_b, cf_ln_g, cf_ln_b, c_w_out, moe_w_group, moe_b_group, moe_w_router, moe_b_router, moe_w1, moe_w3, moe_w2):
    h = jnp.concatenate([meta_tokens, x[0], jnp.zeros((TP - T, D_MODEL), f32)], axis=0)

    w0 = _prep_in0(a_w_in[0], mla_q_norm[0], mla_w_uq[0], mla_kv_norm[0], mla_w_ukv[0])
    qt, k, vt, nq, nk, nv = _in0(h, norm_mix[0][None], w0)
    mla_t = _mla(qt, k, vt).reshape(HEADS * 64, TP)
    nat = _nat(nq, nk, nv, _nat_bias(nat_rpb[0]))
    nat = nat.transpose(1, 0, 2).reshape(TP, HEADS * 64)
    wo = a_w_out[0].astype(bf16)
    wr, br, w13, w2h, ex = _prep_moe(moe_w_group[0], moe_b_group[0], moe_w_router[0], moe_b_router[0],
                                     moe_w1[0], moe_w3[0], moe_w2[0])
    h, xg, gidx = _out_router(h, mla_t, nat, wo[:512], wo[512:], norm_ffn[0][None], wr, br, a_transposed=True)
    h = _moe(xg, gidx, h, w13, w2h, ex)

    hy_in, cf_in = _in1(h, norm_mix[1][None], c_w_in[0].astype(bf16))
    z = _hyena(hy_in, hy_short_w[0], hy_short_b[0], hy_ffn_w1[0], hy_ffn_b1[0], hy_ffn_w2[0], hy_ffn_b2[0],
               hy_ffn_w3[0], hy_sin_freq[0], hy_log_decay[0], hy_bias[0])
    c = _conf(cf_in, cf_dw_w[0], cf_dw_b[0][None], cf_ln_g[0][None], cf_ln_b[0][None])
    wo = c_w_out[0].astype(bf16)
    wr, br, w13, w2h, ex = _prep_moe(moe_w_group[1], moe_b_group[1], moe_w_router[1], moe_b_router[1],
                                     moe_w1[1], moe_w3[1], moe_w2[1])
    h, xg, gidx = _out_router(h, z, c, wo[:512], wo[512:], norm_ffn[1][None], wr, br)
    h = _moe(xg, gidx, h, w13, w2h, ex)

    return _final(h, norm_final[None])[None]
```

```python
import functools
import math

import numpy as np
import jax
import jax.numpy as jnp
from jax import lax
from jax.experimental import pallas as pl
from jax.experimental.pallas import tpu as pltpu

f32 = jnp.float32
bf16 = jnp.bfloat16

D_MODEL = 1024
SEQ = 16384
N_META = 16
T = SEQ + N_META
TM = 512
TP = 33 * TM
NB = TP // TM
EPS = 1e-6
HEADS = 8
MLA_DK = 96
GRID_W = 64
ROWS = SEQ // GRID_W
WIN_ROWS = 8
WIN_COLS = 16
NEG = -1e30
VT_ROWS = 80
NAT_UNROLL = 16
XG_W = D_MODEL + 128
SORT_TILE = 512
SP = (NB + 4) * SORT_TILE
MLA_UNROLL = 16

HY_W = 512
CF_K = 31
N2 = 128
N1 = 286
NFFT = N1 * N2
K1 = N1 // 2 + 1
N1P = 144
THY = N1P * N2

VMEM_LIMIT = 56 * 1024 * 1024


def _cp(sem, vmem=VMEM_LIMIT):
    return pltpu.CompilerParams(dimension_semantics=sem, vmem_limit_bytes=vmem)


def _rms(x, g):
    return x * lax.rsqrt(jnp.mean(x * x, axis=-1, keepdims=True) + EPS) * g


def _dot(a, b):
    return jnp.dot(a, b, preferred_element_type=f32)


def _dot_hi(a, b):
    return jnp.dot(a, b, preferred_element_type=f32, precision=lax.Precision.HIGHEST)


def _k_in0(h_ref, g_ref, win_ref, qn_ref, kvn_ref, wq_ref, wqr_ref, wk_ref, e_ref, er_ref, wv_ref, cos_ref, sin_ref,
           qt_out, k_out, vt_out, nq_out, nk_out, nv_out):
    i = pl.program_id(0)
    hn = _rms(h_ref[...], g_ref[...]).astype(bf16)
    p = _dot(hn, win_ref[...])
    for hd in range(HEADS):
        nq_out[hd] = p[:, 512 + 64 * hd:576 + 64 * hd].astype(bf16)
        nk_out[hd] = p[:, 1024 + 64 * hd:1088 + 64 * hd].astype(bf16)
        nv_out[hd] = p[:, 1536 + 64 * hd:1600 + 64 * hd].astype(bf16)
    cqn = _rms(p[:, 0:256], qn_ref[...]).astype(bf16)
    ckvn = _rms(p[:, 256:384], kvn_ref[...]).astype(bf16)
    kpe = p[:, 384:512].astype(bf16)
    qa = _dot(cqn, wq_ref[...])
    qr = _dot(cqn, wqr_ref[...])
    ka = _dot(ckvn, wk_ref[...]) + _dot(kpe, e_ref[...])
    kr = _dot(kpe, er_ref[...])
    va = _dot(ckvn, wv_ref[...])
    cos = cos_ref[...]
    sin = sin_ref[...]
    t = i * TM + lax.broadcasted_iota(jnp.int32, (TM, 1), 0)
    lane = lax.broadcasted_iota(jnp.int32, (1, 128), 1)
    ones_col = jnp.where((t < T) & (lane == 64), 1.0, 0.0)
    for hd in range(HEADS):
        sl = slice(hd * 128, (hd + 1) * 128)
        qt_out[sl, :] = (qa[:, sl] * cos + qr[:, sl] * sin).T.astype(bf16)
        k_out[hd] = (ka[:, sl] * cos + kr[:, sl] * sin).astype(bf16)
        vt_out[hd, 0] = (va[:, sl] + ones_col).T[:VT_ROWS].astype(bf16)


def _in0(h, g, w):
    full = lambda shape: pl.BlockSpec(shape, lambda i: (0,) * len(shape))
    row = lambda n: pl.BlockSpec((TM, n), lambda i: (i, 0))
    hm = lambda n: pl.BlockSpec((HEADS, TM, n), lambda i: (0, i, 0))
    outs = [jax.ShapeDtypeStruct((HEADS * 128, TP), bf16), jax.ShapeDtypeStruct((HEADS, TP, 128), bf16),
            jax.ShapeDtypeStruct((HEADS, NB, VT_ROWS, TM), bf16)] + [jax.ShapeDtypeStruct((HEADS, TP, 64), bf16)] * 3
    return pl.pallas_call(
        _k_in0,
        grid=(NB,),
        in_specs=[row(1024), full((1, 1024)), full((1024, 2048)), full((1, 256)), full((1, 128)),
                  full((256, 1024)), full((256, 1024)), full((128, 1024)), full((128, 1024)), full((128, 1024)),
                  full((128, 1024)), row(128), row(128)],
        out_specs=[pl.BlockSpec((HEADS * 128, TM), lambda i: (0, i)), hm(128),
                   pl.BlockSpec((HEADS, 1, VT_ROWS, TM), lambda i: (0, i, 0, 0)), hm(64), hm(64), hm(64)],
        out_shape=outs,
        compiler_params=_cp(("parallel",)),
        name="in0",
    )(h, g, w["win"], w["qn"], w["kvn"], w["wq"], w["wqr"], w["wk"], w["e"], w["er"], w["wv"], w["cos"], w["sin"])


def _k_mla(qt_ref, k_ref, vt_ref, o_ref, sa_ref, sb_ref, m_ref, acc_ref):
    qt = qt_ref[...]
    m_ref[...] = jnp.full((1, TM), NEG, f32)
    acc_ref[...] = jnp.zeros((VT_ROWS, TM), f32)

    def scores(j):
        return _dot(k_ref[0, j], qt)

    def update(s_ref, j, masked):
        s = s_ref[...]
        if masked:
            key = lax.broadcasted_iota(jnp.int32, (TM, 1), 0)
            s = jnp.where(key < T - (NB - 1) * TM, s, NEG)
        m = m_ref[...]
        m_new = jnp.maximum(m, s.max(axis=0, keepdims=True))
        p = jnp.exp2(s - m_new).astype(bf16)
        acc_ref[...] = jnp.exp2(m - m_new) * acc_ref[...] + _dot(vt_ref[0, j], p)
        m_ref[...] = m_new

    sa_ref[...] = scores(0)

    def group(i, _):
        for u in range(0, MLA_UNROLL, 2):
            j = MLA_UNROLL * i + u
            sb_ref[...] = scores(j + 1)
            update(sa_ref, j, False)
            sa_ref[...] = scores(j + 2)
            update(sb_ref, j + 1, False)
        return 0

    lax.fori_loop(0, (NB - 1) // MLA_UNROLL, group, 0)
    update(sa_ref, NB - 1, True)
    acc = acc_ref[...]
    o_ref[0] = (acc[:64] / acc[64:65]).astype(bf16)


def _mla(qt, k, vt):
    return pl.pallas_call(
        _k_mla,
        grid=(HEADS, NB),
        in_specs=[pl.BlockSpec((128, TM), lambda h, i: (h, i)),
                  pl.BlockSpec((1, NB, TM, 128), lambda h, i: (h, 0, 0, 0)),
                  pl.BlockSpec((1, NB, VT_ROWS, TM), lambda h, i: (h, 0, 0, 0))],
        out_specs=pl.BlockSpec((1, 64, TM), lambda h, i: (h, 0, i)),
        out_shape=jax.ShapeDtypeStruct((HEADS, 64, TP), bf16),
        scratch_shapes=[pltpu.VMEM((TM, TM), f32), pltpu.VMEM((TM, TM), f32), pltpu.VMEM((1, TM), f32),
                        pltpu.VMEM((VT_ROWS, TM), f32)],
        compiler_params=_cp(("parallel", "parallel")),
        name="mla",
    )(qt, k.reshape(HEADS, NB, TM, 128), vt)


def _k_nat(q_ref, k_ref, v_ref, c_ref, o_ref):
    km = k_ref[0, 0:N_META, :]
    vm = v_ref[0, 0:N_META, :]
    qm = q_ref[0, 0:N_META, :]
    s = lax.dot_general(qm, km, (((1,), (1,)), ((), ())), preferred_element_type=f32)
    p = jnp.exp(s - s.max(axis=1, keepdims=True))
    o_ref[0, 0:N_META, :] = (_dot(p.astype(bf16), vm) / p.sum(axis=1, keepdims=True)).astype(bf16)
    o_ref[0, T:TP, :] = jnp.zeros((TP - T, 64), bf16)

    def rows(ii, _):
        idx = [ii * NAT_UNROLL + r for r in range(NAT_UNROLL)]
        r0s = [jnp.clip(i - WIN_ROWS // 2, 0, ROWS - WIN_ROWS) for i in idx]
        qoffs = [pl.multiple_of(N_META + GRID_W * i, 16) for i in idx]
        koffs = [pl.multiple_of(N_META + GRID_W * r0, 16) for r0 in r0s]
        ss, sms = [], []
        for i, r0, qoff, koff in zip(idx, r0s, qoffs, koffs):
            a0 = r0 - i + WIN_ROWS - 1
            qi = q_ref[0, pl.ds(qoff, GRID_W), :]
            kw = k_ref[0, pl.ds(koff, WIN_ROWS * GRID_W), :]
            s = lax.dot_general(qi, kw, (((1,), (1,)), ((), ())), preferred_element_type=f32)
            ss.append(s + jnp.concatenate([c_ref[0, a0 + 2 * j] for j in range(WIN_ROWS // 2)], axis=1))
            sms.append(lax.dot_general(qi, km, (((1,), (1,)), ((), ())), preferred_element_type=f32))
        ps, pms, ls = [], [], []
        for s, sm in zip(ss, sms):
            m = jnp.maximum(s.max(axis=1, keepdims=True), sm.max(axis=1, keepdims=True))
            p = jnp.exp(s - m)
            pm = jnp.exp(sm - m)
            ls.append(p.sum(axis=1, keepdims=True) + pm.sum(axis=1, keepdims=True))
            ps.append(p.astype(bf16))
            pms.append(pm.astype(bf16))
        for p, pm, l, qoff, koff in zip(ps, pms, ls, qoffs, koffs):
            vw = v_ref[0, pl.ds(koff, WIN_ROWS * GRID_W), :]
            o = _dot(p, vw) + _dot(pm, vm)
            o_ref[0, pl.ds(qoff, GRID_W), :] = (o / l).astype(bf16)
        return 0

    lax.fori_loop(0, ROWS // NAT_UNROLL, rows, 0)


def _nat(q, k, v, c2):
    blk = pl.BlockSpec((1, TP, 64), lambda h: (h, 0, 0))
    return pl.pallas_call(
        _k_nat,
        grid=(HEADS,),
        in_specs=[blk, blk, blk, pl.BlockSpec((1, 2 * WIN_ROWS - 2, GRID_W, 2 * GRID_W), lambda h: (h, 0, 0, 0))],
        out_specs=blk,
        out_shape=jax.ShapeDtypeStruct((HEADS, TP, 64), bf16),
        compiler_params=_cp(("parallel",)),
        name="nat",
    )(q, k, v, c2)


def _k_out_router(h_ref, a_ref, b_ref, wa_ref, wb_ref, g_ref, wr_ref, br_ref, h_out, xg_out, gidx_out, *, a_transposed):
    if a_transposed:
        ya = lax.dot_general(a_ref[...], wa_ref[...], (((0,), (0,)), ((), ())), preferred_element_type=f32)
    else:
        ya = _dot(a_ref[...].reshape(TM, 512).astype(bf16), wa_ref[...])
    h1 = h_ref[...] + ya + _dot(b_ref[...].astype(bf16), wb_ref[...])
    h_out[...] = h1
    xn = _rms(h1, g_ref[...])
    xg_out[:, :D_MODEL] = xn
    xh = xn.astype(bf16)
    xl = (xn - xh.astype(f32)).astype(bf16)
    wr = wr_ref[...]
    wh = wr.astype(bf16)
    wl = (wr - wh.astype(f32)).astype(bf16)
    logits = _dot(xh, wh) + _dot(xh, wl) + _dot(xl, wh) + br_ref[...]
    lane = lax.broadcasted_iota(jnp.int32, logits.shape, 1)
    big = jnp.int32(1 << 20)
    gl = jnp.where((lane >= 32) & (lane < 36), logits, NEG)
    gmax = gl.max(axis=1, keepdims=True)
    gsum = jnp.exp(gl - gmax).sum(axis=1, keepdims=True)
    p_group = 1.0 / gsum
    gidx = jnp.clip(jnp.where(gl == gmax, lane, big).min(axis=1, keepdims=True) - 32, 0, 3)
    el = jnp.where((lane >> 3) == gidx, logits, NEG)
    v1 = el.max(axis=1, keepdims=True)
    i1 = jnp.where(el == v1, lane, big).min(axis=1, keepdims=True)
    el2 = jnp.where(lane == i1, NEG, el)
    v2 = el2.max(axis=1, keepdims=True)
    i2 = jnp.where(el2 == v2, lane, big).min(axis=1, keepdims=True)
    e2 = jnp.exp(v2 - v1)
    w1 = p_group / (1.0 + e2)
    w2 = w1 * e2
    xg_out[:, D_MODEL:] = jnp.where(lane == i1, w1, jnp.where(lane == i2, w2, 0.0))
    gidx_out[...] = gidx


def _out_router(h, a, b, wa, wb, g, wr, br, a_transposed=False):
    full = lambda shape: pl.BlockSpec(shape, lambda i: (0,) * len(shape))
    row = lambda n: pl.BlockSpec((TM, n), lambda i: (i, 0))
    a_spec = (pl.BlockSpec((512, TM), lambda i: (0, i)) if a_transposed
              else pl.BlockSpec((TM // N2, N2, 512), lambda i: (i, 0, 0)))
    return pl.pallas_call(
        functools.partial(_k_out_router, a_transposed=a_transposed),
        grid=(NB,),
        in_specs=[row(1024), a_spec, row(512), full((512, 1024)), full((512, 1024)), full((1, 1024)),
                  full((1024, 128)), full((1, 128))],
        out_specs=[row(1024), row(XG_W), row(1)],
        out_shape=[jax.ShapeDtypeStruct((TP, 1024), f32), jax.ShapeDtypeStruct((TP, XG_W), f32),
                   jax.ShapeDtypeStruct((TP, 1), jnp.int32)],
        compiler_params=_cp(("parallel",)),
        name="out_router",
    )(h, a, b, wa, wb, g, wr, br)


def _k_rank(g_ref, rank_out, cnt_out, carry_ref):
    @pl.when(pl.program_id(0) == 0)
    def _():
        carry_ref[...] = jnp.zeros((1, 128), f32)

    lane = lax.broadcasted_iota(jnp.int32, (TM, 128), 1)
    oh = jnp.where(lane == g_ref[...], 1.0, 0.0)
    r = lax.broadcasted_iota(jnp.int32, (TM, TM), 0)
    c = lax.broadcasted_iota(jnp.int32, (TM, TM), 1)
    earlier = jnp.where(c < r, 1.0, 0.0).astype(bf16)
    excl = _dot(earlier, oh.astype(bf16)) + carry_ref[...]
    rank_out[...] = (oh * excl).sum(axis=1, keepdims=True).astype(jnp.int32)
    carry_ref[...] += oh.sum(axis=0, keepdims=True)
    cnt_out[...] = carry_ref[...]


def _rank(gidx):
    return pl.pallas_call(
        _k_rank,
        grid=(NB,),
        in_specs=[pl.BlockSpec((TM, 1), lambda i: (i, 0))],
        out_specs=[pl.BlockSpec((TM, 1), lambda i: (i, 0)), pl.BlockSpec((1, 128), lambda i: (0, 0))],
        out_shape=[jax.ShapeDtypeStruct((TP, 1), jnp.int32), jax.ShapeDtypeStruct((1, 128), f32)],
        scratch_shapes=[pltpu.VMEM((1, 128), f32)],
        compiler_params=_cp(("arbitrary",)),
        name="moe_rank",
    )(gidx)


def _row_copy(src_ref, src_row, dst_ref, dst_row, sem):
    return pltpu.make_async_copy(src_ref.at[pl.ds(src_row, 1)], dst_ref.at[pl.ds(dst_row, 1)], sem)


def _k_dispatch(pos_ref, x_ref, xs_in_ref, xs_ref, sem):
    del xs_in_ref
    base = pl.program_id(0) * TM

    def start(r8, _):
        for u in range(8):
            r = r8 * 8 + u
            _row_copy(x_ref, r, xs_ref, pos_ref[base + r], sem).start(priority=u % 2)
        return 0

    lax.fori_loop(0, TM // 8, start, 0)
    pltpu.make_async_copy(x_ref, xs_ref.at[pl.ds(0, TM)], sem).wait()


def _dispatch(pos, xg):
    return pl.pallas_call(
        _k_dispatch,
        grid_spec=pltpu.PrefetchScalarGridSpec(
            num_scalar_prefetch=1, grid=(NB,),
            in_specs=[pl.BlockSpec((TM, XG_W), lambda i, pos: (i, 0)), pl.BlockSpec(memory_space=pl.ANY)],
            out_specs=pl.BlockSpec(memory_space=pl.ANY),
            scratch_shapes=[pltpu.SemaphoreType.DMA(())]),
        out_shape=jax.ShapeDtypeStruct((SP, XG_W), f32),
        input_output_aliases={2: 0},
        compiler_params=_cp(("arbitrary",)),
        name="moe_dispatch",
    )(pos, xg, jnp.zeros((SP, XG_W), f32))


def _k_moe(tg_ref, nt_ref, xs_ref, w13_ref, w2_ref, ex_ref, ys_ref):
    del tg_ref
    live = pl.program_id(0) < nt_ref[0]
    half = pl.program_id(1)

    @pl.when(jnp.logical_not(live) & (half == 0))
    def _():
        ys_ref[...] = jnp.zeros((SORT_TILE, D_MODEL), f32)

    @pl.when(live)
    def _():
        x = xs_ref[...]
        a = _dot(x[:, :D_MODEL].astype(bf16), w13_ref[0, 0])
        gate = x[:, D_MODEL:]
        ghi = gate.astype(bf16)
        glo = (gate - ghi.astype(f32)).astype(bf16)
        gx = _dot(ghi, ex_ref[0, 0]) + _dot(glo, ex_ref[0, 0])
        a1 = a[:, :1024]
        hid = a1 * jax.nn.sigmoid(a1) * a[:, 1024:] * gx
        y = _dot(hid.astype(bf16), w2_ref[0, 0])

        @pl.when(half == 0)
        def _():
            ys_ref[...] = y

        @pl.when(half == 1)
        def _():
            ys_ref[...] += y


def _moe_sorted(tile_group, n_tiles, xs, w13, w2, ex):
    wspec = lambda shape: pl.BlockSpec((1, 1) + shape, lambda j, s, tg, nt: (tg[j], s, 0, 0))
    return pl.pallas_call(
        _k_moe,
        grid_spec=pltpu.PrefetchScalarGridSpec(
            num_scalar_prefetch=2, grid=(SP // SORT_TILE, 2),
            in_specs=[pl.BlockSpec((SORT_TILE, XG_W), lambda j, s, tg, nt: (j, 0)),
                      wspec((1024, 2048)), wspec((1024, 1024)), wspec((128, 1024))],
            out_specs=pl.BlockSpec((SORT_TILE, D_MODEL), lambda j, s, tg, nt: (j, 0))),
        out_shape=jax.ShapeDtypeStruct((SP, D_MODEL), f32),
        compiler_params=_cp(("arbitrary", "arbitrary")),
        name="moe",
    )(tile_group, n_tiles, xs, w13, w2, ex)


def _k_combine(pos_ref, h_ref, ys_ref, o_ref, buf_ref, sem):
    base = pl.program_id(0) * TM

    def start(r, _):
        _row_copy(ys_ref, pos_ref[base + r], buf_ref, r, sem).start()
        return 0

    lax.fori_loop(0, TM, start, 0, unroll=8)
    pltpu.make_async_copy(ys_ref.at[pl.ds(0, TM)], buf_ref, sem).wait()
    o_ref[...] = h_ref[...] + buf_ref[...]


def _combine(pos, h, ys):
    return pl.pallas_call(
        _k_combine,
        grid_spec=pltpu.PrefetchScalarGridSpec(
            num_scalar_prefetch=1, grid=(NB,),
            in_specs=[pl.BlockSpec((TM, D_MODEL), lambda i, pos: (i, 0)), pl.BlockSpec(memory_space=pl.ANY)],
            out_specs=pl.BlockSpec((TM, D_MODEL), lambda i, pos: (i, 0)),
            scratch_shapes=[pltpu.VMEM((TM, D_MODEL), f32), pltpu.SemaphoreType.DMA(())]),
        out_shape=jax.ShapeDtypeStruct((TP, D_MODEL), f32),
        compiler_params=_cp(("arbitrary",)),
        name="moe_combine",
    )(pos, h, ys)


def _moe(xg, gidx, h, w13, w2, ex):
    rank, cnt = _rank(gidx)
    cnt = cnt[0, :4].astype(jnp.int32)
    tiles = (cnt + SORT_TILE - 1) // SORT_TILE
    tile_end = jnp.cumsum(tiles)
    start = (tile_end - tiles) * SORT_TILE
    pos = start[gidx[:, 0]] + rank[:, 0]
    tile_group = jnp.minimum(jnp.sum(jnp.arange(SP // SORT_TILE)[:, None] >= tile_end[None, :], axis=1), 3)
    xs = _dispatch(pos, xg)
    ys = _moe_sorted(tile_group.astype(jnp.int32), tile_end[3:4].astype(jnp.int32), xs, w13, w2, ex)
    return _combine(pos, h, ys)


def _k_in1(h_ref, g_ref, w_ref, hy_out, cf_out):
    hn = _rms(h_ref[...], g_ref[...]).astype(bf16)
    p = _dot(hn, w_ref[...])
    hy_out[...] = p[:, :1536]
    cf_out[...] = p[:, 1536:]


def _in1(h, g, w):
    row = lambda n: pl.BlockSpec((TM, n), lambda i: (i, 0))
    return pl.pallas_call(
        _k_in1,
        grid=(NB,),
        in_specs=[row(1024), pl.BlockSpec((1, 1024), lambda i: (0, 0)), pl.BlockSpec((1024, 2560), lambda i: (0, 0))],
        out_specs=[row(1536), row(1024)],
        out_shape=[jax.ShapeDtypeStruct((TP, 1536), f32), jax.ShapeDtypeStruct((TP, 1024), f32)],
        compiler_params=_cp(("parallel",)),
        name="in1",
    )(h, g, w)


def _k_short(prev_ref, cur_ref, next_ref, w_ref, b_ref, v_out, x1_out, x2_out):
    i = pl.program_id(0)
    ext = jnp.concatenate([prev_ref[...], cur_ref[...], next_ref[...]], axis=0)
    t = i * TM - 8 + lax.broadcasted_iota(jnp.int32, (TM + 16, 1), 0)
    ext = jnp.where((t >= 0) & (t < T), ext, 0.0)
    w = w_ref[...]
    s = ext[7:7 + TM] * w[0:1] + ext[8:8 + TM] * w[1:2] + ext[9:9 + TM] * w[2:3] + b_ref[...]
    s = jnp.where(t[8:8 + TM] < T, s, 0.0)
    v_out[...] = s[:, :512].reshape(TM // N2, N2, 512)
    x1_out[...] = s[:, 512:1024].reshape(TM // N2, N2, 512)
    x2_out[...] = s[:, 1024:].reshape(TM // N2, N2, 512)


def _short(hy, w, b):
    last = NB - 1
    nblk = THY // TM
    return pl.pallas_call(
        _k_short,
        grid=(nblk,),
        in_specs=[pl.BlockSpec((8, 1536), lambda i: (jnp.maximum(jnp.minimum(i, last) * (TM // 8) - 1, 0), 0)),
                  pl.BlockSpec((TM, 1536), lambda i: (jnp.minimum(i, last), 0)),
                  pl.BlockSpec((8, 1536), lambda i: (jnp.minimum((jnp.minimum(i, last) + 1) * (TM // 8), TP // 8 - 1), 0)),
                  pl.BlockSpec((3, 1536), lambda i: (0, 0)), pl.BlockSpec((1, 1536), lambda i: (0, 0))],
        out_specs=[pl.BlockSpec((TM // N2, N2, 512), lambda i: (i, 0, 0))] * 3,
        out_shape=[jax.ShapeDtypeStruct((N1P, N2, 512), f32)] * 3,
        compiler_params=_cp(("parallel",)),
        name="hy_short",
    )(hy, hy, hy, w, b)


def _k_filt(feat_ref, w1_ref, b1_ref, w2_ref, b2_ref, w3_ref, fr_ref, rate_ref, o_ref):
    i = pl.program_id(0)
    feats = feat_ref[...]
    fr = fr_ref[...]
    hid = jnp.sin(fr * (_dot_hi(feats, w1_ref[...]) + b1_ref[...]))
    hid = jnp.sin(fr * (_dot_hi(hid, w2_ref[...]) + b2_ref[...]))
    filt = _dot(hid.astype(bf16), w3_ref[...].astype(bf16))
    filt = filt * jnp.exp(-feats[:, 0:1] * rate_ref[...])
    lag = i * TM + lax.broadcasted_iota(jnp.int32, (TM, 1), 0)
    col = lax.broadcasted_iota(jnp.int32, (1, 2048), 1)
    keep = (lag < T) & ((lag > 0) | (col < 1024))
    o_ref[...] = jnp.where(keep, filt, 0.0).reshape(TM // N2, N2, 2048)


def _filt(feats, w1, b1, w2, b2, w3, fr, rate):
    full = lambda shape: pl.BlockSpec(shape, lambda i: (0,) * len(shape))
    return pl.pallas_call(
        _k_filt,
        grid=(THY // TM,),
        in_specs=[pl.BlockSpec((TM, 128), lambda i: (i, 0)), full((128, 64)), full((1, 64)), full((64, 64)),
                  full((1, 64)), full((64, 2048)), full((1, 64)), full((1, 2048))],
        out_specs=pl.BlockSpec((TM // N2, N2, 2048), lambda i: (i, 0, 0)),
        out_shape=jax.ShapeDtypeStruct((N1P, N2, 2048), f32),
        compiler_params=_cp(("parallel",)),
        name="hy_filt",
    )(feats, w1, b1, w2, b2, w3, fr, rate)


def _k_dft1(f_ref, x_ref, y_ref):
    f = f_ref[...]
    for j in range(8):
        y = _dot(f, x_ref[:, j, :].astype(bf16))
        y_ref[:, 0, j, :] = y[:K1]
        y_ref[:, 1, j, :] = y[K1:]


def _dft1(f1c, x3):
    c = x3.shape[2]
    return pl.pallas_call(
        _k_dft1,
        grid=(N2 // 8, c // 512),
        in_specs=[pl.BlockSpec((2 * K1, N1P), lambda i, cb: (0, 0)),
                  pl.BlockSpec((N1P, 8, 512), lambda i, cb: (0, i, cb))],
        out_specs=pl.BlockSpec((K1, 2, 8, 512), lambda i, cb: (0, 0, i, cb)),
        out_shape=jax.ShapeDtypeStruct((K1, 2, N2, c), f32),
        compiler_params=_cp(("parallel", "parallel")),
        name="dft1",
    )(f1c, x3)


def _stage2_matrix(f2_ref, tw_ref):
    fr, fi = f2_ref[0], f2_ref[1]
    tr, ti = tw_ref[0, 0:1, :], tw_ref[0, 1:2, :]
    gre = fr * tr - fi * ti
    gim = fr * ti + fi * tr
    return jnp.concatenate([jnp.concatenate([gre, -gim], axis=1), jnp.concatenate([gim, gre], axis=1)], axis=0)


def _k_spec_filt(y_ref, f2_ref, tw_ref, o_ref):
    y = y_ref[0].reshape(2 * N2, 2048).astype(bf16)
    x = _dot(_stage2_matrix(f2_ref, tw_ref).astype(bf16), y)
    o_ref[0, 0] = x[:N2, :1024] + x[:N2, 1024:]
    o_ref[0, 1] = x[N2:, :1024] - x[N2:, 1024:]


def _spec_filt(yf, f2, tw):
    return pl.pallas_call(
        _k_spec_filt,
        grid=(K1,),
        in_specs=[pl.BlockSpec((1, 2, N2, 2048), lambda k: (k, 0, 0, 0)),
                  pl.BlockSpec((2, N2, N2), lambda k: (0, 0, 0)), pl.BlockSpec((1, 2, N2), lambda k: (k, 0, 0))],
        out_specs=pl.BlockSpec((1, 2, N2, 1024), lambda k: (k, 0, 0, 0)),
        out_shape=jax.ShapeDtypeStruct((K1, 2, N2, 1024), f32),
        compiler_params=_cp(("parallel",)),
        name="spec_filt",
    )(yf, f2, tw)


def _k_spec_conv(y_ref, kf_ref, f2_ref, tw_ref, z_ref):
    g = _stage2_matrix(f2_ref, tw_ref)
    y = y_ref[0].reshape(2 * N2, 512).astype(bf16)
    x = _dot(g.astype(bf16), y)
    xr, xi = x[:N2], x[N2:]
    kr, ki = kf_ref[0, 0], kf_ref[0, 1]
    p = jnp.concatenate([xr * kr - xi * ki, xr * ki + xi * kr], axis=0).astype(bf16)
    z_ref[0] = _dot(g.T.astype(bf16), p).reshape(2, N2, 512)


def _spec_conv(y, kf, order, f2, tw):
    return pl.pallas_call(
        _k_spec_conv,
        grid=(K1,),
        in_specs=[pl.BlockSpec((1, 2, N2, 512), lambda k: (k, 0, 0, 0)),
                  pl.BlockSpec((1, 2, N2, 512), lambda k: (k, 0, 0, order)),
                  pl.BlockSpec((2, N2, N2), lambda k: (0, 0, 0)), pl.BlockSpec((1, 2, N2), lambda k: (k, 0, 0))],
        out_specs=pl.BlockSpec((1, 2, N2, 512), lambda k: (k, 0, 0, 0)),
        out_shape=jax.ShapeDtypeStruct((K1, 2, N2, 512), f32),
        compiler_params=_cp(("parallel",)),
        name="spec_conv",
    )(y, kf, f2, tw)


def _k_idft(f_ref, z_ref, zin_ref, gate_ref, bias_ref, o_ref):
    f = f_ref[...]
    bias = bias_ref[...]
    for j in range(8):
        zj = jnp.concatenate([z_ref[:, 0, j, :], z_ref[:, 1, j, :]], axis=0).astype(bf16)
        y = _dot(f, zj)
        o_ref[:, j, :] = gate_ref[:, j, :] * (y + zin_ref[:, j, :] * bias)


def _idft(f1inv, z, zin, gate, bias):
    blk = pl.BlockSpec((N1P, 8, 512), lambda i: (0, i, 0))
    return pl.pallas_call(
        _k_idft,
        grid=(N2 // 8,),
        in_specs=[pl.BlockSpec((N1P, 2 * K1), lambda i: (0, 0)),
                  pl.BlockSpec((K1, 2, 8, 512), lambda i: (0, 0, i, 0)), blk, blk,
                  pl.BlockSpec((1, 512), lambda i: (0, 0))],
        out_specs=blk,
        out_shape=jax.ShapeDtypeStruct((N1P, N2, 512), f32),
        compiler_params=_cp(("parallel",)),
        name="idft",
    )(f1inv, z, zin, gate, bias)


def _k_conf(prev_ref, cur_ref, next_ref, w_ref, b_ref, lg_ref, lb_ref, o_ref, sh_ref):
    i = pl.program_id(0)
    ext = jnp.concatenate([prev_ref[...], cur_ref[...], next_ref[...]], axis=0)
    t = i * TM - 16 + lax.broadcasted_iota(jnp.int32, (TM + 32, 1), 0)
    u = ext[:, :512] * jax.nn.sigmoid(ext[:, 512:])
    u = jnp.where((t >= 0) & (t < T), u, 0.0)
    w = w_ref[...]
    acc = jnp.zeros((TM, 512), f32) + b_ref[...]
    for b in range(8):
        sh_ref[b] = u[b:b + TM + 24]
    for j in range(CF_K):
        a, b = divmod(j + 1, 8)
        acc = acc + sh_ref[b, 8 * a:8 * a + TM, :] * w[j:j + 1]
    mu = jnp.mean(acc, axis=-1, keepdims=True)
    xc = acc - mu
    y = xc * lax.rsqrt(jnp.mean(xc * xc, axis=-1, keepdims=True) + EPS) * lg_ref[...] + lb_ref[...]
    o_ref[...] = (y * jax.nn.sigmoid(y)).astype(bf16)


def _conf(cf, w, b, lg, lb):
    full = lambda shape: pl.BlockSpec(shape, lambda i: (0,) * len(shape))
    return pl.pallas_call(
        _k_conf,
        grid=(NB,),
        in_specs=[pl.BlockSpec((16, 1024), lambda i: (jnp.maximum(i * (TM // 16) - 1, 0), 0)),
                  pl.BlockSpec((TM, 1024), lambda i: (i, 0)),
                  pl.BlockSpec((16, 1024), lambda i: (jnp.minimum((i + 1) * (TM // 16), TP // 16 - 1), 0)),
                  full((CF_K, 512)), full((1, 512)), full((1, 512)), full((1, 512))],
        out_specs=pl.BlockSpec((TM, 512), lambda i: (i, 0)),
        out_shape=jax.ShapeDtypeStruct((TP, 512), bf16),
        scratch_shapes=[pltpu.VMEM((8, TM + 24, 512), f32)],
        compiler_params=_cp(("parallel",)),
        name="conformer",
    )(cf, cf, cf, w, b, lg, lb)


def _k_final(h_ref, nxt_ref, g_ref, o_ref):
    h = jnp.concatenate([h_ref[N_META:, :], nxt_ref[...]], axis=0)
    o_ref[...] = _rms(h, g_ref[...])


def _final(h, g):
    return pl.pallas_call(
        _k_final,
        grid=(SEQ // TM,),
        in_specs=[pl.BlockSpec((TM, 1024), lambda i: (i, 0)),
                  pl.BlockSpec((N_META, 1024), lambda i: ((i + 1) * (TM // N_META), 0)),
                  pl.BlockSpec((1, 1024), lambda i: (0, 0))],
        out_specs=pl.BlockSpec((TM, 1024), lambda i: (i, 0)),
        out_shape=jax.ShapeDtypeStruct((SEQ, 1024), f32),
        compiler_params=_cp(("parallel",)),
        name="final_norm",
    )(h, h, g)


def _prep_in0(a_w_in, q_norm, w_uq, kv_norm, w_ukv):
    win = jnp.concatenate([a_w_in[:, :416], jnp.zeros((D_MODEL, 96), f32), a_w_in[:, 416:928] * (64 ** -0.5),
                           a_w_in[:, 928:]], axis=1).astype(bf16)
    wq3 = (w_uq * (MLA_DK ** -0.5 * math.log2(math.e))).reshape(256, HEADS, MLA_DK)
    rope = wq3[..., 64:]
    rot = jnp.concatenate([-rope[..., 16:], rope[..., :16]], axis=-1)
    z64 = jnp.zeros((256, HEADS, 64), f32)
    z32 = jnp.zeros((256, HEADS, 32), f32)
    wq = jnp.concatenate([wq3, z32], axis=-1).reshape(256, 1024).astype(bf16)
    wqr = jnp.concatenate([z64, rot, z32], axis=-1).reshape(256, 1024).astype(bf16)
    kv3 = w_ukv.reshape(128, HEADS, 128)
    wk = jnp.concatenate([kv3[..., :64], jnp.zeros((128, HEADS, 64), f32)], axis=-1).reshape(128, 1024).astype(bf16)
    wv = jnp.concatenate([kv3[..., 64:], jnp.zeros((128, HEADS, 64), f32)], axis=-1).reshape(128, 1024).astype(bf16)
    e = np.zeros((128, HEADS, 128), np.float32)
    er = np.zeros((128, HEADS, 128), np.float32)
    for r in range(32):
        e[r, :, 64 + r] = 1.0
        if r < 16:
            er[r + 16, :, 64 + r] = -1.0
        else:
            er[r - 16, :, 64 + r] = 1.0
    pos = jnp.arange(TP, dtype=f32)
    inv_freq = 10000.0 ** (-jnp.arange(0, 32, 2, dtype=f32) / 32)
    ang = pos[:, None] * inv_freq[None, :]
    c, s = jnp.cos(ang), jnp.sin(ang)
    cos = jnp.concatenate([jnp.ones((TP, 64), f32), c, c, jnp.zeros((TP, 32), f32)], axis=1)
    sin = jnp.concatenate([jnp.zeros((TP, 64), f32), s, s, jnp.zeros((TP, 32), f32)], axis=1)
    return dict(win=win, qn=q_norm[None], kvn=kv_norm[None], wq=wq, wqr=wqr, wk=wk,
                e=jnp.asarray(e.reshape(128, 1024)).astype(bf16), er=jnp.asarray(er.reshape(128, 1024)).astype(bf16),
                wv=wv, cos=cos, sin=sin)


def _nat_bias(rpb):
    qc = np.arange(GRID_W)[:, None]
    kc = np.arange(GRID_W)[None, :]
    c0 = np.clip(qc - WIN_COLS // 2, 0, GRID_W - WIN_COLS)
    inside = (kc >= c0) & (kc < c0 + WIN_COLS)
    idx = np.clip(kc - qc + WIN_COLS - 1, 0, 2 * WIN_COLS - 2)
    c = jnp.where(jnp.asarray(inside)[None, None], rpb[:, :, idx], NEG)
    return jnp.concatenate([c[:, :-1], c[:, 1:]], axis=-1)


def _prep_moe(w_group, b_group, w_router, b_router, w1, w3, w2):
    wr = jnp.concatenate([w_router.transpose(1, 0, 2).reshape(D_MODEL, 32), w_group,
                          jnp.zeros((D_MODEL, 92), f32)], axis=1)
    br = jnp.concatenate([b_router.reshape(32), b_group, jnp.zeros((92,), f32)])[None]

    def half(w):
        return w.reshape(4, 2, 4, D_MODEL, 256).transpose(0, 1, 3, 2, 4).reshape(4, 2, D_MODEL, 1024)

    w13 = jnp.concatenate([half(w1), half(w3)], axis=-1).astype(bf16)
    w2h = w2.reshape(4, 2, 1024, D_MODEL).astype(bf16)
    ex = np.zeros((4, 2, 128, 4, 256), np.float32)
    for g in range(4):
        for s in range(2):
            for e in range(4):
                ex[g, s, g * 8 + s * 4 + e, e, :] = 1.0
    return wr, br, w13, w2h, jnp.asarray(ex.reshape(4, 2, 128, 1024)).astype(bf16)


def _dft_tables():
    two_pi = 2.0 * math.pi
    k1 = jnp.arange(K1, dtype=jnp.int32)
    n1 = jnp.arange(N1P, dtype=jnp.int32)
    th = two_pi * ((k1[:, None] * n1[None, :]) % N1).astype(f32) / N1
    f1c = jnp.concatenate([jnp.cos(th), -jnp.sin(th)], axis=0).astype(bf16)
    wgt = jnp.where((k1 == 0) | (k1 == N1 // 2), 1.0, 2.0).astype(f32) / NFFT
    f1inv = jnp.concatenate([jnp.cos(th).T * wgt[None, :], -jnp.sin(th).T * wgt[None, :]], axis=1).astype(bf16)
    n2 = jnp.arange(N2, dtype=jnp.int32)
    a2 = -two_pi * ((n2[:, None] * n2[None, :]) % N2).astype(f32) / N2
    f2 = jnp.stack([jnp.cos(a2), jnp.sin(a2)])
    at = -two_pi * ((k1[:, None] * n2[None, :]) % NFFT).astype(f32) / NFFT
    tw = jnp.stack([jnp.cos(at), jnp.sin(at)], axis=1)
    return f1c, f1inv, f2, tw


def _hyena(hy_in, short_w, short_b, f_w1, f_b1, f_w2, f_b2, f_w3, sin_freq, log_decay, hy_bias):
    v, x1, x2 = _short(hy_in, short_w, short_b[None])
    f1c, f1inv, f2, tw = _dft_tables()
    tt = jnp.linspace(0.0, 1.0, T, dtype=f32)
    bands = jnp.linspace(1e-4, 15.0, 16, dtype=f32)
    ang = 2.0 * math.pi * tt[:, None] * bands[None, :]
    feats = jnp.concatenate([tt[:, None], jnp.cos(ang), jnp.sin(ang), jnp.zeros((T, 95), f32)], axis=-1)
    feats = jnp.concatenate([feats, jnp.zeros((THY - T, 128), f32)], axis=0)
    w1p = jnp.concatenate([f_w1, jnp.zeros((95, 64), f32)], axis=0)
    rate = jnp.exp(log_decay.astype(f32)).reshape(1, 2048)
    filt = _filt(feats, w1p, f_b1[None], f_w2, f_b2[None], f_w3, sin_freq[None], rate)
    kf = _spec_filt(_dft1(f1c, filt), f2, tw)
    z = v
    for o, gate in enumerate((x1, x2)):
        zf = _spec_conv(_dft1(f1c, z), kf, o, f2, tw)
        z = _idft(f1inv, zf, z, gate, hy_bias[o][None])
    return z


def kernel(x, meta_tokens, norm_mix, norm_ffn, norm_final, a_w_in, mla_q_norm, mla_w_uq, mla_kv_norm, mla_w_ukv, nat_rpb, a_w_out, c_w_in, hy_short_w, hy_short_b, hy_ffn_w1, hy_ffn_b1, hy_ffn_w2, hy_ffn_b2, hy_ffn_w3, hy_sin_freq, hy_log_decay, hy_bias, cf_dw_w, cf_dw_b, cf_ln_g, cf_ln_b, c_w_out, moe_w_group, moe_b_group, moe_w_router, moe_b_router, moe_w1, moe_w3, moe_w2):
    h = jnp.concatenate([meta_tokens, x[0], jnp.zeros((TP - T, D_MODEL), f32)], axis=0)

    w0 = _prep_in0(a_w_in[0], mla_q_norm[0], mla_w_uq[0], mla_kv_norm[0], mla_w_ukv[0])
    qt, k, vt, nq, nk, nv = _in0(h, norm_mix[0][None], w0)
    mla_t = _mla(qt, k, vt).reshape(HEADS * 64, TP)
    nat = _nat(nq, nk, nv, _nat_bias(nat_rpb[0]))
    nat = nat.transpose(1, 0, 2).reshape(TP, HEADS * 64)
    wo = a_w_out[0].astype(bf16)
    wr, br, w13, w2h, ex = _prep_moe(moe_w_group[0], moe_b_group[0], moe_w_router[0], moe_b_router[0],
                                     moe_w1[0], moe_w3[0], moe_w2[0])
    h, xg, gidx = _out_router(h, mla_t, nat, wo[:512], wo[512:], norm_ffn[0][None], wr, br, a_transposed=True)
    h = _moe(xg, gidx, h, w13, w2h, ex)

    hy_in, cf_in = _in1(h, norm_mix[1][None], c_w_in[0].astype(bf16))
    z = _hyena(hy_in, hy_short_w[0], hy_short_b[0], hy_ffn_w1[0], hy_ffn_b1[0], hy_ffn_w2[0], hy_ffn_b2[0],
               hy_ffn_w3[0], hy_sin_freq[0], hy_log_decay[0], hy_bias[0])
    c = _conf(cf_in, cf_dw_w[0], cf_dw_b[0][None], cf_ln_g[0][None], cf_ln_b[0][None])
    wo = c_w_out[0].astype(bf16)
    wr, br, w13, w2h, ex = _prep_moe(moe_w_group[1], moe_b_group[1], moe_w_router[1], moe_b_router[1],
                                     moe_w1[1], moe_w3[1], moe_w2[1])
    h, xg, gidx = _out_router(h, z, c, wo[:512], wo[512:], norm_ffn[1][None], wr, br)
    h = _moe(xg, gidx, h, w13, w2h, ex)

    return _final(h, norm_final[None])[None]
```
